```python
import math
import jax, jax.numpy as jnp
from jax import lax
import numpy as np

D_MODEL = 1024
BATCH = 32
SEQ = 256
DEPTH = 4
DEC_BATCH = 8
DEC_SEQ = 2048
PAST_LEN = 256

GRID_W = 64
CHUNK = 128
EPS = 1e-6
A_GROUPS = 8
A_GROUP_DIM = 64
A_WIDTH = A_GROUPS * A_GROUP_DIM
B_HEADS = 8
B_QK_DIM = 64
B_V_DIM = 2 * B_QK_DIM
B_QK_WIDTH = B_HEADS * 2 * B_QK_DIM
B_WIDTH = B_HEADS * B_V_DIM
ROPE_THETA = 10000.0
ROPE_AXIS_DIM = B_QK_DIM // 2
ROPE_FREQS = ROPE_AXIS_DIM // 2
C_GROUPS = 4
C_GROUP_DIM = 128
C_WIDTH = C_GROUPS * C_GROUP_DIM
N_BRANCHES = 3
SPLIT_SIZES = (A_WIDTH, A_WIDTH, B_QK_WIDTH, B_QK_WIDTH, B_WIDTH, C_WIDTH, N_BRANCHES * D_MODEL)
IN_COLS = A_WIDTH * 2 + B_QK_WIDTH * 2 + B_WIDTH + C_WIDTH + N_BRANCHES * D_MODEL
N_EXPERTS = 16
N_EXPERT_GROUPS = 4
EXPERTS_PER_GROUP = N_EXPERTS // N_EXPERT_GROUPS
TOP_K = 2
D_EXPERT = 512

kernel_name = "hybrid_flow_prefix_diffattn_gmlp_fnet_moe"


def rms_norm(x, w):
    xf = x.astype(jnp.float32)
    y = xf * lax.rsqrt(jnp.mean(xf * xf, axis=-1, keepdims=True) + EPS)
    return (y * w.astype(jnp.float32)).astype(x.dtype)


def axial_rope_tables(n_tokens):
    grid_rows = n_tokens // GRID_W
    row = jnp.repeat(jnp.arange(grid_rows), GRID_W).astype(jnp.float32)
    col = jnp.tile(jnp.arange(GRID_W), grid_rows).astype(jnp.float32)
    freq = ROPE_THETA ** (-jnp.arange(ROPE_FREQS, dtype=jnp.float32) / ROPE_FREQS)
    ang = jnp.stack([row[:, None] * freq[None, :], col[:, None] * freq[None, :]], axis=1)
    return jnp.cos(ang), jnp.sin(ang)


def apply_axial_rope(x, cos, sin):
    shp = x.shape
    xr = x.astype(jnp.float32).reshape(shp[:-1] + (2, 2, ROPE_FREQS))
    x1, x2 = xr[..., 0, :], xr[..., 1, :]
    cb = cos[None, :, None, None, :, :]
    sb = sin[None, :, None, None, :, :]
    out = jnp.stack([x1 * cb - x2 * sb, x2 * cb + x1 * sb], axis=-2)
    return out.reshape(shp).astype(x.dtype)


def chunk_gmlp(u, v, norm_w, w_s, b_s):
    B, L, _ = u.shape
    u = jax.nn.gelu(u)
    v = rms_norm(jax.nn.gelu(v), norm_w)
    vc = v.reshape(B, L // CHUNK, CHUNK, A_GROUPS, A_GROUP_DIM)
    sv = jnp.einsum('gpq,bnqgc->bnpgc', w_s, vc) + b_s.T[None, None, :, :, None]
    return u * sv.reshape(B, L, A_WIDTH)


def diff_attention(q, k, v, lam):
    B, Lq = q.shape[:2]
    nb = Lq // CHUNK
    scale = 1.0 / math.sqrt(B_QK_DIM)
    qb = q.reshape(B, nb, CHUNK, B_HEADS, 2, B_QK_DIM).transpose(1, 0, 2, 3, 4, 5)
    kf = k.astype(jnp.float32)
    vf = v.astype(jnp.float32)

    def block(qi):
        s = jnp.einsum('bqhmd,bkhmd->bhmqk', qi.astype(jnp.float32), kf) * scale
        p = jax.nn.softmax(s, axis=-1)
        a = p[:, :, 0] - lam * p[:, :, 1]
        return jnp.einsum('bhqk,bkhe->bqhe', a, vf)

    out = lax.map(block, qb)
    return out.transpose(1, 0, 2, 3, 4).reshape(B, Lq, B_HEADS, B_V_DIM).astype(v.dtype)


def fourier_mix(f):
    B, L, _ = f.shape
    ff = f.astype(jnp.float32).reshape(B, L, C_GROUPS, C_GROUP_DIM)
    out = jnp.fft.fft2(ff, axes=(1, 3), norm='ortho').real
    return out.reshape(B, L, C_WIDTH).astype(f.dtype)


def grouped_moe(h, router_w, router_b, w_g, w_u, w_d):
    T = h.shape[0]
    s = jax.nn.sigmoid(jnp.dot(h.astype(jnp.float32), router_w.astype(jnp.float32)))
    sb = (s + router_b.astype(jnp.float32)).reshape(T, N_EXPERT_GROUPS, EXPERTS_PER_GROUP)
    pair, _ = lax.top_k(sb, TOP_K)
    grp = jnp.argmax(pair.sum(-1), axis=-1)
    within = jnp.take_along_axis(sb, grp[:, None, None], axis=1)[:, 0]
    _, local = lax.top_k(within, TOP_K)
    eidx = grp[:, None] * EXPERTS_PER_GROUP + local
    w = jnp.take_along_axis(s, eidx, axis=1)
    w = w / jnp.sum(w, axis=-1, keepdims=True)
    gates = jnp.einsum('tk,tke->te', w, jax.nn.one_hot(eidx, N_EXPERTS, dtype=jnp.float32))
    y = jnp.zeros(h.shape, jnp.float32)
    for e in range(N_EXPERTS):
        a = jax.nn.silu(h @ w_g[e]) * (h @ w_u[e])
        y = y + gates[:, e:e + 1] * (a @ w_d[e]).astype(jnp.float32)
    return y.astype(h.dtype)


def trunk_layer(x, mod, lp, layer_idx, router_w, router_b, rope=None, ctx_k=None, ctx_v=None):
    B, L, D = x.shape
    shift1, scale1, gate1, shift2, scale2, gate2 = jnp.split(mod, 6, axis=-1)
    h = rms_norm(x, lp['norm1_w']) * (1 + scale1) + shift1
    z = h @ lp['w_in']
    u_a, v_a, q, k, v, f_in, g = jnp.split(z, list(np.cumsum(SPLIT_SIZES)[:-1]), axis=-1)
    y_a = chunk_gmlp(u_a, v_a, lp['a_norm_w'], lp['a_w_s'], lp['a_b_s'])
    q = rms_norm(q.reshape(B, L, B_HEADS, 2, B_QK_DIM), lp['q_norm_w'])
    k = rms_norm(k.reshape(B, L, B_HEADS, 2, B_QK_DIM), lp['k_norm_w'])
    v = v.reshape(B, L, B_HEADS, B_V_DIM)
    if rope is not None:
        q = apply_axial_rope(q, *rope)
        k = apply_axial_rope(k, *rope)
    lam_init = 0.8 - 0.6 * math.exp(-0.3 * layer_idx)
    lq = lp['lambda_qk'].astype(jnp.float32)
    lam = jnp.exp(jnp.sum(lq[0] * lq[1])) - jnp.exp(jnp.sum(lq[2] * lq[3])) + lam_init
    if ctx_k is not None:
        keys = jnp.concatenate([ctx_k, k], axis=1)
        vals = jnp.concatenate([ctx_v, v], axis=1)
    else:
        keys, vals = k, v
    o = diff_attention(q, keys, vals, lam)
    o = rms_norm(o, lp['subln_w']) * (1 - lam_init)
    y_b = o.reshape(B, L, B_WIDTH)
    y_c = fourier_mix(f_in)
    gs = jax.nn.sigmoid(g.reshape(B, L, N_BRANCHES, D))
    merged = (gs[:, :, 0] * (y_a @ lp['w_br_a']) + gs[:, :, 1] * (y_b @ lp['w_br_b'])
              + gs[:, :, 2] * (y_c @ lp['w_br_c']))
    x = x + gate1 * (merged @ lp['w_out'])
    h2 = rms_norm(x, lp['norm2_w']) * (1 + scale2) + shift2
    y = grouped_moe(h2.reshape(B * L, D), router_w, router_b, lp['w_e_gate'], lp['w_e_up'], lp['w_e_down'])
    x = x + gate2 * y.reshape(B, L, D)
    return x, k.reshape(B, L, B_HEADS, 2 * B_QK_DIM), v


def setup_inputs(seed: int = 0) -> dict:
    key = jax.random.key(seed)
    ks = iter(jax.random.split(key, 40))
    f32 = jnp.float32

    def nrm(shape, scale=1.0):
        return jax.random.normal(next(ks), shape, f32) * scale

    D = D_MODEL
    return {
        'x_prompt': nrm((BATCH, SEQ, D)),
        'x_sample': nrm((DEC_BATCH, DEC_SEQ, D)),
        'cache_k': nrm((DEC_BATCH, DEPTH, PAST_LEN, B_HEADS, 2 * B_QK_DIM)),
        'cache_v': nrm((DEC_BATCH, DEPTH, PAST_LEN, B_HEADS, B_V_DIM)),
        'c': nrm((DEC_BATCH, D)),
        'c_ctx': nrm((D,)),
        'w_mod': nrm((DEPTH, D, 6 * D), 0.1 * D ** -0.5),
        'b_mod': nrm((DEPTH, 6 * D), 0.02),
        'norm1_w': 1.0 + nrm((DEPTH, D), 0.02),
        'norm2_w': 1.0 + nrm((DEPTH, D), 0.02),
        'w_in': nrm((DEPTH, D, IN_COLS), D ** -0.5),
        'a_norm_w': 1.0 + nrm((DEPTH, A_WIDTH), 0.02),
        'a_w_s': nrm((DEPTH, A_GROUPS, CHUNK, CHUNK), CHUNK ** -0.5),
        'a_b_s': 1.0 + nrm((DEPTH, A_GROUPS, CHUNK), 0.1),
        'q_norm_w': 1.0 + nrm((DEPTH, B_QK_DIM), 0.02),
        'k_norm_w': 1.0 + nrm((DEPTH, B_QK_DIM), 0.02),
        'lambda_qk': nrm((DEPTH, 4, B_QK_DIM), 0.1),
        'subln_w': 1.0 + nrm((DEPTH, B_V_DIM), 0.02),
        'w_br_a': nrm((DEPTH, A_WIDTH, D), A_WIDTH ** -0.5),
        'w_br_b': nrm((DEPTH, B_WIDTH, D), B_WIDTH ** -0.5),
        'w_br_c': nrm((DEPTH, C_WIDTH, D), C_WIDTH ** -0.5),
        'w_out': nrm((DEPTH, D, D), D ** -0.5),
        'router_w': nrm((D, N_EXPERTS), D ** -0.5),
        'router_b': nrm((N_EXPERTS,), 0.01),
        'w_e_gate': nrm((DEPTH, N_EXPERTS, D, D_EXPERT), D ** -0.5),
        'w_e_up': nrm((DEPTH, N_EXPERTS, D, D_EXPERT), D ** -0.5),
        'w_e_down': nrm((DEPTH, N_EXPERTS, D_EXPERT, D), D_EXPERT ** -0.5),
    }


def reference(x_prompt, x_sample, cache_k, cache_v, c, c_ctx, w_mod, b_mod, norm1_w, norm2_w,
              w_in, a_norm_w, a_w_s, a_b_s, q_norm_w, k_norm_w, lambda_qk, subln_w,
              w_br_a, w_br_b, w_br_c, w_out, router_w, router_b, w_e_gate, w_e_up, w_e_down):
    rope = axial_rope_tables(x_sample.shape[1])
    xp, xs = x_prompt, x_sample
    new_k, new_v = [], []
    for l in range(DEPTH):
        lp = dict(norm1_w=norm1_w[l], norm2_w=norm2_w[l], w_in=w_in[l], a_norm_w=a_norm_w[l],
                  a_w_s=a_w_s[l], a_b_s=a_b_s[l], q_norm_w=q_norm_w[l], k_norm_w=k_norm_w[l],
                  lambda_qk=lambda_qk[l], subln_w=subln_w[l], w_br_a=w_br_a[l], w_br_b=w_br_b[l],
                  w_br_c=w_br_c[l], w_out=w_out[l], w_e_gate=w_e_gate[l], w_e_up=w_e_up[l],
                  w_e_down=w_e_down[l])
        mod_ctx = (jax.nn.silu(c_ctx) @ w_mod[l] + b_mod[l])[None, None, :]
        mod_lat = (jax.nn.silu(c) @ w_mod[l] + b_mod[l])[:, None, :]
        xp, k_l, v_l = trunk_layer(xp, mod_ctx, lp, l, router_w, router_b)
        new_k.append(k_l)
        new_v.append(v_l)
        ck = cache_k[:, l].reshape(cache_k.shape[0], cache_k.shape[2], B_HEADS, 2, B_QK_DIM)
        cv = cache_v[:, l]
        xs, _, _ = trunk_layer(xs, mod_lat, lp, l, router_w, router_b, rope=rope, ctx_k=ck, ctx_v=cv)
    new_k_arr = jnp.stack(new_k, axis=1)
    new_v_arr = jnp.stack(new_v, axis=1)
    return (xp, xs, new_k_arr, new_v_arr)
```

```python
import functools
import math

import jax
import jax.numpy as jnp
from jax import lax
from jax.experimental import pallas as pl
from jax.experimental.pallas import tpu as pltpu

F32 = jnp.float32
BF16 = jnp.bfloat16

EPS = 1e-6
GRID_W = 64
CHUNK = 128
A_GROUPS = 8
A_GROUP_DIM = 64
A_WIDTH = A_GROUPS * A_GROUP_DIM
B_HEADS = 8
B_QK_DIM = 64
B_V_DIM = 2 * B_QK_DIM
B_WIDTH = B_HEADS * B_V_DIM
ROPE_THETA = 10000.0
ROPE_FREQS = B_QK_DIM // 4
C_GROUPS = 4
C_GROUP_DIM = 128
C_WIDTH = C_GROUPS * C_GROUP_DIM
N_EXPERTS = 16
N_EXPERT_GROUPS = 4
EXPERTS_PER_GROUP = N_EXPERTS // N_EXPERT_GROUPS
TOP_K = 2

COL_BLK = 512
CB_U, CB_V, CB_Q, CB_K, CB_VAL, CB_F, CB_G = 0, 1, 2, 4, 6, 8, 9

LANES = 128
VMEM_LIMIT = 56 * 1024 * 1024

NT_DIMS = (((1,), (1,)), ((), ()))


def _params(*sem):
    return pltpu.CompilerParams(dimension_semantics=sem, vmem_limit_bytes=VMEM_LIMIT)


def _dot(a, b):
    return jnp.dot(a, b, preferred_element_type=F32)


def _split_bf16(a):
    hi = a.astype(BF16)
    lo = (a - hi.astype(F32)).astype(BF16)
    return hi, lo


def _gelu_tanh(x):
    return 0.5 * x * (1.0 + jnp.tanh(0.7978845608028654 * (x + 0.044715 * (x * x * x))))


def _sigmoid(x):
    return 1.0 / (1.0 + jnp.exp(-x))


def _rms(x):
    return x * lax.rsqrt(jnp.mean(x * x, axis=-1, keepdims=True) + EPS)


def _mod_kernel(c_ref, w_ref, b_ref, o_ref):
    c = c_ref[...]
    a = c * _sigmoid(c)
    ah, al = _split_bf16(a)
    wh, wl = _split_bf16(w_ref[0])
    o_ref[0] = _dot(ah, wh) + _dot(al, wh) + _dot(ah, wl) + b_ref[0]


def _mod_vectors(cvec, w_mod, b_mod):
    depth, d, n = w_mod.shape
    r = cvec.shape[0]
    tn = 1536
    return pl.pallas_call(
        _mod_kernel,
        grid=(depth, n // tn),
        in_specs=[pl.BlockSpec((r, d), lambda l, j: (0, 0)),
                  pl.BlockSpec((1, d, tn), lambda l, j: (l, 0, j)),
                  pl.BlockSpec((1, 1, tn), lambda l, j: (l, 0, j))],
        out_specs=pl.BlockSpec((1, r, tn), lambda l, j: (l, 0, j)),
        out_shape=jax.ShapeDtypeStruct((depth, r, n), F32),
        compiler_params=_params("parallel", "parallel"),
        name="mod_vectors",
    )(cvec, w_mod, b_mod.reshape(depth, 1, n))


def _in_proj_kernel(x_ref, mod_ref, nw_ref, w_ref, z_ref, h_scr):
    @pl.when(pl.program_id(1) == 0)
    def _():
        y = _rms(x_ref[...]) * nw_ref[...]
        h_scr[...] = (y * (1.0 + mod_ref[0, 1:2, :]) + mod_ref[0, 0:1, :]).astype(BF16)

    z_ref[...] = _dot(h_scr[...], w_ref[...]).astype(z_ref.dtype)


def _in_proj(x, mod, mod_row, norm_w, w_in, tm):
    t, d = x.shape
    n = w_in.shape[1]
    return pl.pallas_call(
        _in_proj_kernel,
        grid=(t // tm, n // COL_BLK),
        in_specs=[pl.BlockSpec((tm, d), lambda i, j: (i, 0)),
                  pl.BlockSpec((1, 6, d), lambda i, j: (mod_row(i), 0, 0)),
                  pl.BlockSpec((1, d), lambda i, j: (0, 0)),
                  pl.BlockSpec((d, COL_BLK), lambda i, j: (0, j))],
        out_specs=pl.BlockSpec((tm, COL_BLK), lambda i, j: (i, j)),
        out_shape=jax.ShapeDtypeStruct((t, n), BF16),
        scratch_shapes=[pltpu.VMEM((tm, d), BF16)],
        compiler_params=_params("parallel", "arbitrary"),
        name="in_proj",
    )(x, mod, norm_w, w_in)


def _mixer_a_kernel(u_ref, v_ref, nw_ref, ws_ref, bias_ref, o_ref):
    tm = u_ref.shape[0]
    v = _rms(_gelu_tanh(v_ref[...].astype(F32))) * nw_ref[...]
    vb = v.astype(BF16)
    lo = lax.broadcasted_iota(jnp.int32, (1, LANES), 1) < A_GROUP_DIM
    for c in range(tm // CHUNK):
        rows = slice(c * CHUNK, (c + 1) * CHUNK)
        for j in range(A_WIDTH // LANES):
            cols = slice(j * LANES, (j + 1) * LANES)
            blk = vb[rows, cols]
            sv = jnp.where(lo, _dot(ws_ref[2 * j], blk), _dot(ws_ref[2 * j + 1], blk)) + bias_ref[:, cols]
            u = _gelu_tanh(u_ref[rows, cols].astype(F32))
            o_ref[rows, cols] = (u * sv).astype(o_ref.dtype)


def _mixer_a(z, norm_w, w_s, bias, tm):
    t = z.shape[0]
    return pl.pallas_call(
        _mixer_a_kernel,
        grid=(t // tm,),
        in_specs=[pl.BlockSpec((tm, A_WIDTH), lambda i: (i, CB_U)),
                  pl.BlockSpec((tm, A_WIDTH), lambda i: (i, CB_V)),
                  pl.BlockSpec((1, A_WIDTH), lambda i: (0, 0)),
                  pl.BlockSpec((A_GROUPS, CHUNK, CHUNK), lambda i: (0, 0, 0)),
                  pl.BlockSpec((CHUNK, A_WIDTH), lambda i: (0, 0))],
        out_specs=pl.BlockSpec((tm, A_WIDTH), lambda i: (i, 0)),
        out_shape=jax.ShapeDtypeStruct((t, A_WIDTH), BF16),
        compiler_params=_params("parallel"),
        name="mixer_a",
    )(z, z, norm_w, w_s, bias)


def _group_rms(x, lo):
    sq = x * x
    s_lo = jnp.sum(jnp.where(lo, sq, 0.0), axis=-1, keepdims=True)
    s_hi = jnp.sum(jnp.where(lo, 0.0, sq), axis=-1, keepdims=True)
    inv = jnp.where(lo, lax.rsqrt(s_lo * (1.0 / B_QK_DIM) + EPS), lax.rsqrt(s_hi * (1.0 / B_QK_DIM) + EPS))
    return x * inv


def _qk_prep_kernel(*refs, rope, emit_k32):
    zq_ref, zk_ref, qw_ref, kw_ref = refs[:4]
    pos = 4
    if rope:
        cos_ref, sin_ref = refs[pos:pos + 2]
        pos += 2
    q_ref, k_ref = refs[pos:pos + 2]
    k32_ref = refs[pos + 2] if emit_k32 else None

    lane = lax.broadcasted_iota(jnp.int32, (1, LANES), 1)
    lo = lane < B_QK_DIM
    first_half = (lane % (2 * ROPE_FREQS)) < ROPE_FREQS
    q_scale = 1.0 / math.sqrt(B_QK_DIM)

    def prep(x, w):
        y = _group_rms(x.astype(F32), lo) * w
        if rope:
            swapped = jnp.where(first_half, pltpu.roll(y, LANES - ROPE_FREQS, axis=1),
                                pltpu.roll(y, ROPE_FREQS, axis=1))
            y = y * cos_ref[...] + swapped * sin_ref[...]
        return y

    for h in range(B_HEADS):
        cols = slice(h * LANES, (h + 1) * LANES)
        q = prep(zq_ref[:, cols], qw_ref[...])
        q_ref[:, cols] = (q * q_scale).astype(q_ref.dtype)
        k = prep(zk_ref[:, cols], kw_ref[...])
        k_ref[:, cols] = k.astype(k_ref.dtype)
        if emit_k32:
            k32_ref[:, cols] = k


def _qk_prep(z, row0, rows, qw, kw, rope_tabs, emit_k32, tm):
    rb0 = row0 // tm
    width = B_HEADS * LANES
    wq, wk = CB_Q * COL_BLK // width, CB_K * COL_BLK // width
    in_specs = [pl.BlockSpec((tm, width), lambda i: (rb0 + i, wq)),
                pl.BlockSpec((tm, width), lambda i: (rb0 + i, wk)),
                pl.BlockSpec((1, LANES), lambda i: (0, 0)),
                pl.BlockSpec((1, LANES), lambda i: (0, 0))]
    args = [z, z, qw, kw]
    if rope_tabs is not None:
        seq_blocks = rope_tabs[0].shape[0] // tm
        in_specs += [pl.BlockSpec((tm, LANES), lambda i: (i % seq_blocks, 0))] * 2
        args += list(rope_tabs)
    out_specs = [pl.BlockSpec((tm, width), lambda i: (i, 0))] * (3 if emit_k32 else 2)
    out_shape = [jax.ShapeDtypeStruct((rows, width), BF16)] * 2
    if emit_k32:
        out_shape.append(jax.ShapeDtypeStruct((rows, width), F32))
    return pl.pallas_call(
        functools.partial(_qk_prep_kernel, rope=rope_tabs is not None, emit_k32=emit_k32),
        grid=(rows // tm,),
        in_specs=in_specs,
        out_specs=out_specs,
        out_shape=out_shape,
        compiler_params=_params("parallel"),
        name="qk_prep",
    )(*args)


def _rope_tables(n_tokens):
    t = jnp.arange(n_tokens)
    row = (t // GRID_W).astype(F32)
    col = (t % GRID_W).astype(F32)
    freq = ROPE_THETA ** (-jnp.arange(ROPE_FREQS, dtype=F32) / ROPE_FREQS)
    ang_r = row[:, None] * freq[None, :]
    ang_c = col[:, None] * freq[None, :]
    cos = jnp.concatenate([jnp.cos(ang_r)] * 2 + [jnp.cos(ang_c)] * 2, axis=1)
    sin = jnp.concatenate([-jnp.sin(ang_r), jnp.sin(ang_r), -jnp.sin(ang_c), jnp.sin(ang_c)], axis=1)
    return jnp.tile(cos, (1, 2)), jnp.tile(sin, (1, 2))


def _attn_kernel(*refs, lam_init, has_ctx):
    lq_ref, sw_ref, q_ref, k_ref, v_ref = refs[:5]
    if has_ctx:
        ck_ref, cv_ref, o_ref = refs[5:]
    else:
        (o_ref,) = refs[5:]

    lq = lq_ref[...]
    s01 = jnp.sum(lq[0:1] * lq[1:2], axis=-1, keepdims=True)
    s23 = jnp.sum(lq[2:3] * lq[3:4], axis=-1, keepdims=True)
    lam = jnp.exp(s01) - jnp.exp(s23) + lam_init

    q = q_ref[...]
    lo = lax.broadcasted_iota(jnp.int32, (1, LANES), 1) < B_QK_DIM
    zero = jnp.zeros_like(q)
    q_maps = (jnp.where(lo, q, zero), jnp.where(lo, zero, q))
    k = k_ref[...]
    if has_ctx:
        ck = ck_ref[...].astype(BF16)

    weights = []
    for m in range(2):
        s_l = lax.dot_general(q_maps[m], k, NT_DIMS, preferred_element_type=F32)
        mx = jnp.max(s_l, axis=-1, keepdims=True)
        if has_ctx:
            s_c = lax.dot_general(q_maps[m], ck, NT_DIMS, preferred_element_type=F32)
            mx = jnp.maximum(mx, jnp.max(s_c, axis=-1, keepdims=True))
        e_l = jnp.exp(s_l - mx)
        den = jnp.sum(e_l, axis=-1, keepdims=True)
        if has_ctx:
            e_c = jnp.exp(s_c - mx)
            den = den + jnp.sum(e_c, axis=-1, keepdims=True)
        r = (1.0 if m == 0 else lam) / den
        weights.append((e_l * r, e_c * r if has_ctx else None))

    a_l = (weights[0][0] - weights[1][0]).astype(BF16)
    o = _dot(a_l, v_ref[...])
    if has_ctx:
        a_c = (weights[0][1] - weights[1][1]).astype(BF16)
        o = o + _dot(a_c, cv_ref[...].astype(BF16))
    o = _rms(o) * (sw_ref[...] * (1.0 - lam_init))
    o_ref[...] = o.astype(o_ref.dtype)


def _attention(q, k, z, row0, batch, seq, lam_qk, subln_w, lam_init, ctx, tq):
    nq = seq // tq
    vcol0 = CB_VAL * COL_BLK // LANES
    vrow0 = row0 // seq
    in_specs = [pl.BlockSpec((4, B_QK_DIM), lambda b, h, i: (0, 0)),
                pl.BlockSpec((1, LANES), lambda b, h, i: (0, 0)),
                pl.BlockSpec((tq, LANES), lambda b, h, i: (b * nq + i, h)),
                pl.BlockSpec((seq, LANES), lambda b, h, i: (b, h)),
                pl.BlockSpec((seq, LANES), lambda b, h, i: (vrow0 + b, vcol0 + h))]
    args = [lam_qk, subln_w, q, k, z]
    if ctx is not None:
        cache_k, cache_v, layer = ctx
        past = cache_k.shape[2]
        spec = pl.BlockSpec((None, None, past, LANES), lambda b, h, i: (b, layer, 0, h))
        in_specs += [spec, spec]
        args += [cache_k, cache_v]
    return pl.pallas_call(
        functools.partial(_attn_kernel, lam_init=lam_init, has_ctx=ctx is not None),
        grid=(batch, B_HEADS, nq),
        in_specs=in_specs,
        out_specs=pl.BlockSpec((tq, LANES), lambda b, h, i: (b * nq + i, h)),
        out_shape=jax.ShapeDtypeStruct((batch * seq, B_WIDTH), BF16),
        compiler_params=_params("parallel", "parallel", "arbitrary"),
        name="diff_attention",
    )(*args)


def _dft_tables(seq):
    def cos_sin(n):
        idx = jnp.arange(n, dtype=jnp.int32)
        ang = ((idx[:, None] * idx[None, :]) % n).astype(F32) * (2.0 * math.pi / n)
        s = 1.0 / math.sqrt(n)
        return jnp.cos(ang) * s, jnp.sin(ang) * s

    c_l, s_l = cos_sin(seq)
    c_c, s_c = cos_sin(C_GROUP_DIM)
    eye = jnp.eye(C_GROUPS, dtype=F32)
    chan = jnp.concatenate([jnp.kron(eye, c_c), jnp.kron(eye, s_c)], axis=1)
    return jnp.concatenate([c_l, -s_l], axis=1).astype(BF16), chan.astype(BF16)


def _fourier_kernel(f_ref, chan_ref, w_ref, o_ref, xcs_scr):
    seq = f_ref.shape[0]

    @pl.when(pl.program_id(1) == 0)
    def _():
        t = _dot(f_ref[...], chan_ref[...])
        xcs_scr[0:seq, :] = t[:, :C_WIDTH].astype(BF16)
        xcs_scr[seq:2 * seq, :] = t[:, C_WIDTH:].astype(BF16)

    o_ref[...] = _dot(w_ref[...], xcs_scr[...]).astype(o_ref.dtype)


def _fourier(z, row0, batch, seq, tables, tr):
    w, chan = tables
    nr = seq // tr
    rb0 = row0 // seq
    return pl.pallas_call(
        _fourier_kernel,
        grid=(batch, nr),
        in_specs=[pl.BlockSpec((seq, C_WIDTH), lambda b, r: (rb0 + b, CB_F)),
                  pl.BlockSpec((C_WIDTH, 2 * C_WIDTH), lambda b, r: (0, 0)),
                  pl.BlockSpec((tr, 2 * seq), lambda b, r: (r, 0))],
        out_specs=pl.BlockSpec((tr, C_WIDTH), lambda b, r: (b * nr + r, 0)),
        out_shape=jax.ShapeDtypeStruct((batch * seq, C_WIDTH), BF16),
        scratch_shapes=[pltpu.VMEM((2 * seq, C_WIDTH), BF16)],
        compiler_params=_params("parallel", "arbitrary"),
        name="fourier_mix",
    )(z, chan, w)


def _route(sb, s):
    rows_b = [sb[e:e + 1, :] for e in range(N_EXPERTS)]
    rows_s = [s[e:e + 1, :] for e in range(N_EXPERTS)]
    best, grp = None, None
    for g in range(N_EXPERT_GROUPS):
        r = rows_b[g * EXPERTS_PER_GROUP:(g + 1) * EXPERTS_PER_GROUP]
        pair = None
        for a in range(EXPERTS_PER_GROUP):
            for b in range(a + 1, EXPERTS_PER_GROUP):
                pair = r[a] + r[b] if pair is None else jnp.maximum(pair, r[a] + r[b])
        if g == 0:
            best, grp = pair, jnp.zeros(pair.shape, jnp.int32)
        else:
            upd = pair > best
            grp = jnp.where(upd, g, grp)
            best = jnp.where(upd, pair, best)
    within_b, within_s = [], []
    for j in range(EXPERTS_PER_GROUP):
        wb, ws = rows_b[j], rows_s[j]
        for g in range(1, N_EXPERT_GROUPS):
            wb = jnp.where(grp == g, rows_b[g * EXPERTS_PER_GROUP + j], wb)
            ws = jnp.where(grp == g, rows_s[g * EXPERTS_PER_GROUP + j], ws)
        within_b.append(wb)
        within_s.append(ws)

    def argmax_first(skip):
        bv, bi, bs = None, None, None
        for j in range(EXPERTS_PER_GROUP):
            v = within_b[j] if skip is None else jnp.where(skip == j, -jnp.inf, within_b[j])
            if j == 0:
                bv, bi, bs = v, jnp.zeros(v.shape, jnp.int32), within_s[0]
            else:
                upd = v > bv
                bv = jnp.where(upd, v, bv)
                bi = jnp.where(upd, j, bi)
                bs = jnp.where(upd, within_s[j], bs)
        return bi, bs

    i1, s1 = argmax_first(None)
    i2, s2 = argmax_first(i1)
    tot = s1 + s2
    eidx = jnp.concatenate([grp * EXPERTS_PER_GROUP + i1, grp * EXPERTS_PER_GROUP + i2], axis=0)
    wts = jnp.concatenate([s1 / tot, s2 / tot], axis=0)
    return eidx, wts


def _merge_kernel(x_ref, mod_ref, ya_ref, yb_ref, yc_ref, g_refs, wa_ref, wb_ref, wc_ref, wo_ref,
                  nw_ref, rw_ref, rb_ref, x1_ref, h2_ref, eidx_ref, wts_ref):
    half = COL_BLK
    pa = _dot(ya_ref[...], wa_ref[...])
    pb = _dot(yb_ref[...], wb_ref[...])
    pc = _dot(yc_ref[...], wc_ref[...])
    parts = []
    for c in range(2):
        cols = slice(c * half, (c + 1) * half)
        m = (_sigmoid(g_refs[0 + c][...].astype(F32)) * pa[:, cols]
             + _sigmoid(g_refs[2 + c][...].astype(F32)) * pb[:, cols]
             + _sigmoid(g_refs[4 + c][...].astype(F32)) * pc[:, cols])
        parts.append(m.astype(BF16))
    merged = jnp.concatenate(parts, axis=1)
    x1 = x_ref[...] + mod_ref[0, 2:3, :] * _dot(merged, wo_ref[...])
    x1_ref[...] = x1
    h2 = _rms(x1) * nw_ref[...] * (1.0 + mod_ref[0, 4:5, :]) + mod_ref[0, 3:4, :]
    h2_ref[...] = h2.astype(h2_ref.dtype)
    hh, hl = _split_bf16(h2)
    rh, rl = _split_bf16(rw_ref[...])
    logits = (lax.dot_general(rh, hh, NT_DIMS, preferred_element_type=F32)
              + lax.dot_general(rh, hl, NT_DIMS, preferred_element_type=F32)
              + lax.dot_general(rl, hh, NT_DIMS, preferred_element_type=F32))
    s = _sigmoid(logits)
    eidx, wts = _route(s + rb_ref[...], s)
    eidx_ref[...] = eidx
    wts_ref[...] = wts


def _merge(x, mod, mod_row, ya, yb, yc, z, w_br_a, w_br_b, w_br_c, w_out, norm2_w, router_wt, router_b, tm):
    t, d = x.shape
    const = lambda shape: pl.BlockSpec(shape, lambda i: (0,) * len(shape))
    g_specs = [pl.BlockSpec((tm, COL_BLK), functools.partial(lambda i, cb: (i, cb), cb=CB_G + n)) for n in range(6)]

    def body(x_ref, mod_ref, ya_ref, yb_ref, yc_ref, g0, g1, g2, g3, g4, g5, *rest):
        _merge_kernel(x_ref, mod_ref, ya_ref, yb_ref, yc_ref, (g0, g1, g2, g3, g4, g5), *rest)

    return pl.pallas_call(
        body,
        grid=(t // tm,),
        in_specs=[pl.BlockSpec((tm, d), lambda i: (i, 0)),
                  pl.BlockSpec((1, 6, d), lambda i: (mod_row(i), 0, 0)),
                  pl.BlockSpec((tm, A_WIDTH), lambda i: (i, 0)),
                  pl.BlockSpec((tm, B_WIDTH), lambda i: (i, 0)),
                  pl.BlockSpec((tm, C_WIDTH), lambda i: (i, 0)),
                  *g_specs,
                  const(w_br_a.shape), const(w_br_b.shape), const(w_br_c.shape), const(w_out.shape),
                  const((1, d)), const(router_wt.shape), const((N_EXPERTS, 1))],
        out_specs=[pl.BlockSpec((tm, d), lambda i: (i, 0)),
                   pl.BlockSpec((tm, d), lambda i: (i, 0)),
                   pl.BlockSpec((TOP_K, tm), lambda i: (0, i)),
                   pl.BlockSpec((TOP_K, tm), lambda i: (0, i))],
        out_shape=[jax.ShapeDtypeStruct((t, d), F32),
                   jax.ShapeDtypeStruct((t, d), BF16),
                   jax.ShapeDtypeStruct((TOP_K, t), jnp.int32),
                   jax.ShapeDtypeStruct((TOP_K, t), F32)],
        compiler_params=_params("parallel"),
        name="merge_router",
    )(x, mod, ya, yb, yc, z, z, z, z, z, z, w_br_a, w_br_b, w_br_c, w_out, norm2_w, router_wt, router_b)


def _expert_kernel(te_ref, nu_ref, h_ref, wg_ref, wu_ref, wd_ref, y_ref):
    used = pl.program_id(0) < nu_ref[0]

    @pl.when(used)
    def _():
        h = h_ref[...]
        g = _dot(h, wg_ref[...])
        a = (g * _sigmoid(g)) * _dot(h, wu_ref[...])
        y_ref[...] = _dot(a.astype(BF16), wd_ref[...]).astype(y_ref.dtype)

    @pl.when(jnp.logical_not(used))
    def _():
        y_ref[...] = jnp.zeros_like(y_ref)


def _experts(hg, tile_expert, n_used, w_g, w_u, w_d, tm):
    p, d = hg.shape
    de = w_g.shape[-1]
    grid_spec = pltpu.PrefetchScalarGridSpec(
        num_scalar_prefetch=2,
        grid=(p // tm,),
        in_specs=[pl.BlockSpec((tm, d), lambda i, te, nu: (i, 0)),
                  pl.BlockSpec((None, d, de), lambda i, te, nu: (te[i], 0, 0)),
                  pl.BlockSpec((None, d, de), lambda i, te, nu: (te[i], 0, 0)),
                  pl.BlockSpec((None, de, d), lambda i, te, nu: (te[i], 0, 0))],
        out_specs=pl.BlockSpec((tm, d), lambda i, te, nu: (i, 0)),
    )
    return pl.pallas_call(
        _expert_kernel,
        grid_spec=grid_spec,
        out_shape=jax.ShapeDtypeStruct((p, d), BF16),
        compiler_params=_params("arbitrary"),
        name="experts",
    )(tile_expert, n_used, hg, w_g, w_u, w_d)


def _dispatch_plan(eidx, tm):
    t = eidx.shape[1]
    n_assign = TOP_K * t
    flat_e = eidx.reshape(n_assign)
    onehot = (flat_e[:, None] == jnp.arange(N_EXPERTS, dtype=jnp.int32)[None, :]).astype(jnp.int32)
    csum = jnp.cumsum(onehot, axis=0)
    rank = jnp.take_along_axis(csum, flat_e[:, None], axis=1)[:, 0] - 1
    counts = csum[-1]
    tiles_per = (counts + tm - 1) // tm
    tile_end = jnp.cumsum(tiles_per)
    row_start = (tile_end - tiles_per) * tm
    dest = row_start[flat_e] + rank
    n_tiles = (n_assign + N_EXPERTS * (tm - 1)) // tm
    tok = jnp.arange(n_assign, dtype=jnp.int32) % t
    src_tok = jnp.zeros((n_tiles * tm,), jnp.int32).at[dest].set(tok)
    tile_expert = jnp.minimum(jnp.searchsorted(tile_end, jnp.arange(n_tiles, dtype=jnp.int32), side='right'),
                              N_EXPERTS - 1).astype(jnp.int32)
    n_used = tile_end[-1:].astype(jnp.int32)
    return src_tok, dest.reshape(TOP_K, t), tile_expert, n_used


def _combine_kernel(x_ref, mod_ref, y0_ref, y1_ref, w_ref, o_ref):
    w = w_ref[...]
    y = w[:, 0:1] * y0_ref[...].astype(F32) + w[:, 1:2] * y1_ref[...].astype(F32)
    o_ref[...] = x_ref[...] + mod_ref[0, 5:6, :] * y


def _combine(x, mod, mod_row, y0, y1, w_cols, tm):
    t, d = x.shape
    return pl.pallas_call(
        _combine_kernel,
        grid=(t // tm,),
        in_specs=[pl.BlockSpec((tm, d), lambda i: (i, 0)),
                  pl.BlockSpec((1, 6, d), lambda i: (mod_row(i), 0, 0)),
                  pl.BlockSpec((tm, d), lambda i: (i, 0)),
                  pl.BlockSpec((tm, d), lambda i: (i, 0)),
                  pl.BlockSpec((tm, TOP_K), lambda i: (i, 0))],
        out_specs=pl.BlockSpec((tm, d), lambda i: (i, 0)),
        out_shape=jax.ShapeDtypeStruct((t, d), F32),
        compiler_params=_params("parallel"),
        name="combine",
    )(x, mod, y0, y1, w_cols)


def _pick_tile(preferred, *sizes):
    tile = preferred
    while any(s % tile for s in sizes):
        tile //= 2
    return tile


def kernel(x_prompt, x_sample, cache_k, cache_v, c, c_ctx, w_mod, b_mod, norm1_w, norm2_w, w_in, a_norm_w, a_w_s, a_b_s, q_norm_w, k_norm_w, lambda_qk, subln_w, w_br_a, w_br_b, w_br_c, w_out, router_w, router_b, w_e_gate, w_e_up, w_e_down):
    batch, seq, d = x_prompt.shape
    dec_batch, dec_seq, _ = x_sample.shape
    depth = w_mod.shape[0]
    past = cache_k.shape[2]
    tp, ts = batch * seq, dec_batch * dec_seq
    t = tp + ts
    assert tp % dec_seq == 0 and dec_seq % seq == 0 and seq % CHUNK == 0 and dec_seq % GRID_W == 0

    tm = _pick_tile(1024, tp, dec_seq)
    tm_small = _pick_tile(512, tp, dec_seq)
    tm_exp = 512

    def mod_row_for(tile):
        first_lat = tp // tile
        per_batch = dec_seq // tile
        return lambda i: jnp.where(i < first_lat, 0, 1 + (i - first_lat) // per_batch)

    n_rows = 1 + dec_batch
    r_pad = -(-n_rows // 8) * 8
    cvec = jnp.zeros((r_pad, d), F32).at[0].set(c_ctx).at[1:n_rows].set(c)
    mod_all = _mod_vectors(cvec, w_mod, b_mod)[:, :n_rows].reshape(depth, n_rows, 6, d)

    rope_tabs = _rope_tables(dec_seq)
    dft_p = _dft_tables(seq)
    dft_s = _dft_tables(dec_seq)
    cache_k2 = cache_k.reshape(dec_batch, depth, past, B_HEADS * 2 * B_QK_DIM)
    cache_v2 = cache_v.reshape(dec_batch, depth, past, B_WIDTH)
    router_wt = router_w.T
    router_b2 = router_b.reshape(N_EXPERTS, 1)

    x = jnp.concatenate([x_prompt.reshape(tp, d), x_sample.reshape(ts, d)], axis=0)
    new_k, new_v = [], []
    for l in range(depth):
        mod = mod_all[l]
        lam_init = 0.8 - 0.6 * math.exp(-0.3 * l)
        z = _in_proj(x, mod, mod_row_for(tm), norm1_w[l].reshape(1, d), w_in[l].astype(BF16), tm)

        bias_a = jnp.repeat(a_b_s[l].T, A_GROUP_DIM, axis=1)
        ya = _mixer_a(z, a_norm_w[l].reshape(1, A_WIDTH), a_w_s[l].astype(BF16), bias_a, tm_small)

        qw = jnp.tile(q_norm_w[l], 2).reshape(1, LANES)
        kw = jnp.tile(k_norm_w[l], 2).reshape(1, LANES)
        sw = subln_w[l].reshape(1, LANES)
        tq_p = _pick_tile(512, tp)
        qp, kp, k32 = _qk_prep(z, 0, tp, qw, kw, None, True, tq_p)
        tq_s = _pick_tile(512, dec_seq)
        qs, ks = _qk_prep(z, tp, ts, qw, kw, rope_tabs, False, tq_s)
        yb_p = _attention(qp, kp, z, 0, batch, seq, lambda_qk[l], sw, lam_init, None, seq)
        yb_s = _attention(qs, ks, z, tp, dec_batch, dec_seq, lambda_qk[l], sw, lam_init,
                          (cache_k2, cache_v2, l), _pick_tile(256, dec_seq))
        yb = jnp.concatenate([yb_p, yb_s], axis=0)

        yc_p = _fourier(z, 0, batch, seq, dft_p, seq)
        yc_s = _fourier(z, tp, dec_batch, dec_seq, dft_s, _pick_tile(512, dec_seq))
        yc = jnp.concatenate([yc_p, yc_s], axis=0)

        x1, h2, eidx, wts = _merge(x, mod, mod_row_for(tm_small), ya, yb, yc, z,
                                   w_br_a[l].astype(BF16), w_br_b[l].astype(BF16), w_br_c[l].astype(BF16),
                                   w_out[l].astype(BF16), norm2_w[l].reshape(1, d), router_wt, router_b2, tm_small)

        src_tok, dest, tile_expert, n_used = _dispatch_plan(eidx, tm_exp)
        hg = jnp.take(h2, src_tok, axis=0)
        yg = _experts(hg, tile_expert, n_used, w_e_gate[l].astype(BF16), w_e_up[l].astype(BF16),
                      w_e_down[l].astype(BF16), tm_exp)
        y0 = jnp.take(yg, dest[0], axis=0)
        y1 = jnp.take(yg, dest[1], axis=0)
        x = _combine(x1, mod, mod_row_for(tm_small), y0, y1, wts.T, tm_small)

        new_k.append(k32.reshape(batch, seq, B_HEADS, 2 * B_QK_DIM))
        v_cols = z[:tp, CB_VAL * COL_BLK:CB_VAL * COL_BLK + B_WIDTH]
        new_v.append(v_cols.astype(F32).reshape(batch, seq, B_HEADS, B_V_DIM))

    y_prompt = x[:tp].reshape(batch, seq, d)
    y_sample = x[tp:].reshape(dec_batch, dec_seq, d)
    return (y_prompt, y_sample, jnp.stack(new_k, axis=1), jnp.stack(new_v, axis=1))
```

```python
import functools
import math

import jax
import jax.numpy as jnp
from jax import lax
from jax.experimental import pallas as pl
from jax.experimental.pallas import tpu as pltpu

F32 = jnp.float32
BF16 = jnp.bfloat16

EPS = 1e-6
GRID_W = 64
CHUNK = 128
A_GROUPS = 8
A_GROUP_DIM = 64
A_WIDTH = A_GROUPS * A_GROUP_DIM
B_HEADS = 8
B_QK_DIM = 64
B_V_DIM = 2 * B_QK_DIM
B_WIDTH = B_HEADS * B_V_DIM
ROPE_THETA = 10000.0
ROPE_FREQS = B_QK_DIM // 4
C_GROUPS = 4
C_GROUP_DIM = 128
C_WIDTH = C_GROUPS * C_GROUP_DIM
N_EXPERTS = 16
N_EXPERT_GROUPS = 4
EXPERTS_PER_GROUP = N_EXPERTS // N_EXPERT_GROUPS
TOP_K = 2

COL_BLK = 512
CB_U, CB_V, CB_Q, CB_K, CB_VAL, CB_F, CB_G = 0, 1, 2, 4, 6, 8, 9

LOG2_E = 1.4426950408889634
LANES = 128
SUBLANES = 8
VMEM_LIMIT = 56 * 1024 * 1024

NT_DIMS = (((1,), (1,)), ((), ()))


def _params(*sem):
    return pltpu.CompilerParams(dimension_semantics=sem, vmem_limit_bytes=VMEM_LIMIT)


def _dot(a, b):
    return jnp.dot(a, b, preferred_element_type=F32)


def _split_bf16(a):
    hi = a.astype(BF16)
    lo = (a - hi.astype(F32)).astype(BF16)
    return hi, lo


def _gelu_tanh(x):
    return 0.5 * x * (1.0 + jnp.tanh(0.7978845608028654 * (x + 0.044715 * (x * x * x))))


def _sigmoid(x):
    return 1.0 / (1.0 + jnp.exp(-x))


def _rms(x):
    return x * lax.rsqrt(jnp.mean(x * x, axis=-1, keepdims=True) + EPS)


def _mod_kernel(c_ref, w_ref, b_ref, o_ref):
    c = c_ref[...]
    a = c * _sigmoid(c)
    ah, al = _split_bf16(a)
    wh, wl = _split_bf16(w_ref[0])
    o_ref[0] = _dot(ah, wh) + _dot(al, wh) + _dot(ah, wl) + b_ref[0]


def _mod_vectors(cvec, w_mod, b_mod):
    depth, d, n = w_mod.shape
    r = cvec.shape[0]
    tn = 1536
    return pl.pallas_call(
        _mod_kernel,
        grid=(depth, n // tn),
        in_specs=[pl.BlockSpec((r, d), lambda l, j: (0, 0)),
                  pl.BlockSpec((1, d, tn), lambda l, j: (l, 0, j)),
                  pl.BlockSpec((1, 1, tn), lambda l, j: (l, 0, j))],
        out_specs=pl.BlockSpec((1, r, tn), lambda l, j: (l, 0, j)),
        out_shape=jax.ShapeDtypeStruct((depth, r, n), F32),
        compiler_params=_params("parallel", "parallel"),
        name="mod_vectors",
    )(cvec, w_mod, b_mod.reshape(depth, 1, n))


def _in_proj_kernel(x_ref, mod_ref, nw_ref, w_ref, z_ref, h_scr):
    @pl.when(pl.program_id(1) == 0)
    def _():
        y = _rms(x_ref[...]) * nw_ref[...]
        h_scr[...] = (y * (1.0 + mod_ref[0, 1:2, :]) + mod_ref[0, 0:1, :]).astype(BF16)

    z_ref[...] = _dot(h_scr[...], w_ref[...]).astype(z_ref.dtype)


def _in_proj(x, mod, mod_row, norm_w, w_in, tm, tn):
    t, d = x.shape
    n = w_in.shape[1]
    return pl.pallas_call(
        _in_proj_kernel,
        grid=(t // tm, n // tn),
        in_specs=[pl.BlockSpec((tm, d), lambda i, j: (i, 0)),
                  pl.BlockSpec((1, 6, d), lambda i, j: (mod_row(i), 0, 0)),
                  pl.BlockSpec((1, d), lambda i, j: (0, 0)),
                  pl.BlockSpec((d, tn), lambda i, j: (0, j))],
        out_specs=pl.BlockSpec((tm, tn), lambda i, j: (i, j)),
        out_shape=jax.ShapeDtypeStruct((t, n), BF16),
        scratch_shapes=[pltpu.VMEM((tm, d), BF16)],
        compiler_params=_params("parallel", "arbitrary"),
        name="in_proj",
    )(x, mod, norm_w, w_in)


def _mixer_a_kernel(u_ref, v_ref, nw_ref, ws_ref, bias_ref, o_ref):
    tm = u_ref.shape[0]
    v = _rms(_gelu_tanh(v_ref[...].astype(F32))) * nw_ref[...]
    vb = v.astype(BF16)
    lo = lax.broadcasted_iota(jnp.int32, (1, LANES), 1) < A_GROUP_DIM
    for c in range(tm // CHUNK):
        rows = slice(c * CHUNK, (c + 1) * CHUNK)
        for j in range(A_WIDTH // LANES):
            cols = slice(j * LANES, (j + 1) * LANES)
            blk = vb[rows, cols]
            sv = jnp.where(lo, _dot(ws_ref[2 * j], blk), _dot(ws_ref[2 * j + 1], blk)) + bias_ref[:, cols]
            u = _gelu_tanh(u_ref[rows, cols].astype(F32))
            o_ref[rows, cols] = (u * sv).astype(o_ref.dtype)


def _mixer_a(z, norm_w, w_s, bias, tm):
    t = z.shape[0]
    return pl.pallas_call(
        _mixer_a_kernel,
        grid=(t // tm,),
        in_specs=[pl.BlockSpec((tm, A_WIDTH), lambda i: (i, CB_U)),
                  pl.BlockSpec((tm, A_WIDTH), lambda i: (i, CB_V)),
                  pl.BlockSpec((1, A_WIDTH), lambda i: (0, 0)),
                  pl.BlockSpec((A_GROUPS, CHUNK, CHUNK), lambda i: (0, 0, 0)),
                  pl.BlockSpec((CHUNK, A_WIDTH), lambda i: (0, 0))],
        out_specs=pl.BlockSpec((tm, A_WIDTH), lambda i: (i, 0)),
        out_shape=jax.ShapeDtypeStruct((t, A_WIDTH), BF16),
        compiler_params=_params("parallel"),
        name="mixer_a",
    )(z, z, norm_w, w_s, bias)


def _group_rms(x, lo):
    sq = x * x
    s_lo = jnp.sum(jnp.where(lo, sq, 0.0), axis=-1, keepdims=True)
    s_hi = jnp.sum(jnp.where(lo, 0.0, sq), axis=-1, keepdims=True)
    inv = jnp.where(lo, lax.rsqrt(s_lo * (1.0 / B_QK_DIM) + EPS), lax.rsqrt(s_hi * (1.0 / B_QK_DIM) + EPS))
    return x * inv


def _qk_prep_kernel(*refs, rope, emit_k32):
    zq_ref, zk_ref, qw_ref, kw_ref = refs[:4]
    pos = 4
    if rope:
        cos_ref, sin_ref = refs[pos:pos + 2]
        pos += 2
    q_ref, k_ref = refs[pos:pos + 2]
    k32_ref = refs[pos + 2] if emit_k32 else None

    lane = lax.broadcasted_iota(jnp.int32, (1, LANES), 1)
    lo = lane < B_QK_DIM
    first_half = (lane % (2 * ROPE_FREQS)) < ROPE_FREQS
    q_scale = LOG2_E / math.sqrt(B_QK_DIM)

    def prep(x, w):
        y = _group_rms(x.astype(F32), lo) * w
        if rope:
            swapped = jnp.where(first_half, pltpu.roll(y, LANES - ROPE_FREQS, axis=1),
                                pltpu.roll(y, ROPE_FREQS, axis=1))
            y = y * cos_ref[...] + swapped * sin_ref[...]
        return y

    for h in range(B_HEADS):
        cols = slice(h * LANES, (h + 1) * LANES)
        q = prep(zq_ref[:, cols], qw_ref[...])
        q_ref[:, cols] = (q * q_scale).astype(q_ref.dtype)
        k = prep(zk_ref[:, cols], kw_ref[...])
        k_ref[:, cols] = k.astype(k_ref.dtype)
        if emit_k32:
            k32_ref[:, cols] = k


def _qk_prep(z, row0, rows, qw, kw, rope_tabs, emit_k32, tm):
    rb0 = row0 // tm
    width = B_HEADS * LANES
    wq, wk = CB_Q * COL_BLK // width, CB_K * COL_BLK // width
    in_specs = [pl.BlockSpec((tm, width), lambda i: (rb0 + i, wq)),
                pl.BlockSpec((tm, width), lambda i: (rb0 + i, wk)),
                pl.BlockSpec((1, LANES), lambda i: (0, 0)),
                pl.BlockSpec((1, LANES), lambda i: (0, 0))]
    args = [z, z, qw, kw]
    if rope_tabs is not None:
        seq_blocks = rope_tabs[0].shape[0] // tm
        in_specs += [pl.BlockSpec((tm, LANES), lambda i: (i % seq_blocks, 0))] * 2
        args += list(rope_tabs)
    out_specs = [pl.BlockSpec((tm, width), lambda i: (i, 0))] * (3 if emit_k32 else 2)
    out_shape = [jax.ShapeDtypeStruct((rows, width), BF16)] * 2
    if emit_k32:
        out_shape.append(jax.ShapeDtypeStruct((rows, width), F32))
    return pl.pallas_call(
        functools.partial(_qk_prep_kernel, rope=rope_tabs is not None, emit_k32=emit_k32),
        grid=(rows // tm,),
        in_specs=in_specs,
        out_specs=out_specs,
        out_shape=out_shape,
        compiler_params=_params("parallel"),
        name="qk_prep",
    )(*args)


def _rope_tables(n_tokens):
    t = jnp.arange(n_tokens)
    row = (t // GRID_W).astype(F32)
    col = (t % GRID_W).astype(F32)
    freq = ROPE_THETA ** (-jnp.arange(ROPE_FREQS, dtype=F32) / ROPE_FREQS)
    ang_r = row[:, None] * freq[None, :]
    ang_c = col[:, None] * freq[None, :]
    cos = jnp.concatenate([jnp.cos(ang_r)] * 2 + [jnp.cos(ang_c)] * 2, axis=1)
    sin = jnp.concatenate([-jnp.sin(ang_r), jnp.sin(ang_r), -jnp.sin(ang_c), jnp.sin(ang_c)], axis=1)
    return jnp.tile(cos, (1, 2)), jnp.tile(sin, (1, 2))


def _attn_kernel(*refs, lam_init, has_ctx, heads):
    lq_ref, sw_ref, q_ref, k_ref, v_ref = refs[:5]
    if has_ctx:
        ck_ref, cv_ref, o_ref, vt_scr, cvt_scr = refs[5:]
    else:
        o_ref, vt_scr = refs[5:]

    @pl.when(pl.program_id(2) == 0)
    def _():
        vt_scr[...] = v_ref[...].astype(F32).T.astype(BF16)
        if has_ctx:
            cvt_scr[...] = cv_ref[...].T.astype(BF16)

    lq = lq_ref[...]
    s01 = jnp.sum(lq[0:1] * lq[1:2], axis=-1, keepdims=True)
    s23 = jnp.sum(lq[2:3] * lq[3:4], axis=-1, keepdims=True)
    lam = jnp.exp(s01) - jnp.exp(s23) + lam_init
    lo = lax.broadcasted_iota(jnp.int32, (1, LANES), 1) < B_QK_DIM

    for h in range(heads):
        cols = slice(h * LANES, (h + 1) * LANES)
        q = q_ref[:, cols]
        k = k_ref[:, cols]
        zero = jnp.zeros_like(q)
        outs = []
        for m in range(2):
            qm = jnp.where(lo, q, zero) if m == 0 else jnp.where(lo, zero, q)
            s_l = lax.dot_general(k, qm, NT_DIMS, preferred_element_type=F32)
            mx = jnp.max(s_l, axis=0, keepdims=True)
            if has_ctx:
                s_c = lax.dot_general(ck_ref[:, cols].astype(BF16), qm, NT_DIMS, preferred_element_type=F32)
                mx = jnp.maximum(mx, jnp.max(s_c, axis=0, keepdims=True))
            e_l = jnp.exp2(s_l - mx)
            den = jnp.sum(e_l, axis=0, keepdims=True)
            o_m = _dot(vt_scr[cols, :], e_l.astype(BF16))
            if has_ctx:
                e_c = jnp.exp2(s_c - mx)
                den = den + jnp.sum(e_c, axis=0, keepdims=True)
                o_m = o_m + _dot(cvt_scr[cols, :], e_c.astype(BF16))
            outs.append(o_m * ((1.0 if m == 0 else lam) / den))
        o_t = outs[0] - outs[1]
        o_t = o_t * lax.rsqrt(jnp.mean(o_t * o_t, axis=0, keepdims=True) + EPS)
        o_ref[:, cols] = (o_t.T * (sw_ref[...] * (1.0 - lam_init))).astype(o_ref.dtype)


def _attention(q, k, z, row0, batch, seq, lam_qk, subln_w, lam_init, ctx, tq, heads):
    nq = seq // tq
    width = heads * LANES
    vcol0 = CB_VAL * COL_BLK // width
    vrow0 = row0 // seq
    in_specs = [pl.BlockSpec((4, B_QK_DIM), lambda b, h, i: (0, 0)),
                pl.BlockSpec((1, LANES), lambda b, h, i: (0, 0)),
                pl.BlockSpec((tq, width), lambda b, h, i: (b * nq + i, h)),
                pl.BlockSpec((seq, width), lambda b, h, i: (b, h)),
                pl.BlockSpec((seq, width), lambda b, h, i: (vrow0 + b, vcol0 + h))]
    args = [lam_qk, subln_w, q, k, z]
    scratch = [pltpu.VMEM((width, seq), BF16)]
    if ctx is not None:
        cache_k, cache_v, layer = ctx
        past = cache_k.shape[2]
        spec = pl.BlockSpec((None, None, past, width), lambda b, h, i: (b, layer, 0, h))
        in_specs += [spec, spec]
        args += [cache_k, cache_v]
        scratch.append(pltpu.VMEM((width, past), BF16))
    return pl.pallas_call(
        functools.partial(_attn_kernel, lam_init=lam_init, has_ctx=ctx is not None, heads=heads),
        grid=(batch, B_HEADS // heads, nq),
        in_specs=in_specs,
        out_specs=pl.BlockSpec((tq, width), lambda b, h, i: (b * nq + i, h)),
        out_shape=jax.ShapeDtypeStruct((batch * seq, B_WIDTH), BF16),
        scratch_shapes=scratch,
        compiler_params=_params("parallel", "parallel", "arbitrary"),
        name="diff_attention",
    )(*args)


def _dft_tables(seq):
    def cos_sin(n):
        idx = jnp.arange(n, dtype=jnp.int32)
        ang = ((idx[:, None] * idx[None, :]) % n).astype(F32) * (2.0 * math.pi / n)
        s = 1.0 / math.sqrt(n)
        return jnp.cos(ang) * s, jnp.sin(ang) * s

    c_l, s_l = cos_sin(seq)
    c_c, s_c = cos_sin(C_GROUP_DIM)
    eye = jnp.eye(C_GROUPS, dtype=F32)
    chan = jnp.concatenate([jnp.kron(eye, c_c), jnp.kron(eye, s_c)], axis=1)
    return jnp.concatenate([c_l, -s_l], axis=1).astype(BF16), chan.astype(BF16)


def _fourier_kernel(f_ref, chan_ref, w_ref, o_ref, xcs_scr):
    seq = f_ref.shape[0]

    @pl.when(pl.program_id(1) == 0)
    def _():
        t = _dot(f_ref[...], chan_ref[...])
        xcs_scr[0:seq, :] = t[:, :C_WIDTH].astype(BF16)
        xcs_scr[seq:2 * seq, :] = t[:, C_WIDTH:].astype(BF16)

    o_ref[...] = _dot(w_ref[...], xcs_scr[...]).astype(o_ref.dtype)


def _fourier(z, row0, batch, seq, tables, tr):
    w, chan = tables
    nr = seq // tr
    rb0 = row0 // seq
    return pl.pallas_call(
        _fourier_kernel,
        grid=(batch, nr),
        in_specs=[pl.BlockSpec((seq, C_WIDTH), lambda b, r: (rb0 + b, CB_F)),
                  pl.BlockSpec((C_WIDTH, 2 * C_WIDTH), lambda b, r: (0, 0)),
                  pl.BlockSpec((tr, 2 * seq), lambda b, r: (r, 0))],
        out_specs=pl.BlockSpec((tr, C_WIDTH), lambda b, r: (b * nr + r, 0)),
        out_shape=jax.ShapeDtypeStruct((batch * seq, C_WIDTH), BF16),
        scratch_shapes=[pltpu.VMEM((2 * seq, C_WIDTH), BF16)],
        compiler_params=_params("parallel", "arbitrary"),
        name="fourier_mix",
    )(z, chan, w)


def _route(sb, s):
    rows_b = [sb[e:e + 1, :] for e in range(N_EXPERTS)]
    rows_s = [s[e:e + 1, :] for e in range(N_EXPERTS)]
    best, grp = None, None
    for g in range(N_EXPERT_GROUPS):
        r = rows_b[g * EXPERTS_PER_GROUP:(g + 1) * EXPERTS_PER_GROUP]
        pair = None
        for a in range(EXPERTS_PER_GROUP):
            for b in range(a + 1, EXPERTS_PER_GROUP):
                pair = r[a] + r[b] if pair is None else jnp.maximum(pair, r[a] + r[b])
        if g == 0:
            best, grp = pair, jnp.zeros(pair.shape, jnp.int32)
        else:
            upd = pair > best
            grp = jnp.where(upd, g, grp)
            best = jnp.where(upd, pair, best)
    within_b, within_s = [], []
    for j in range(EXPERTS_PER_GROUP):
        wb, ws = rows_b[j], rows_s[j]
        for g in range(1, N_EXPERT_GROUPS):
            wb = jnp.where(grp == g, rows_b[g * EXPERTS_PER_GROUP + j], wb)
            ws = jnp.where(grp == g, rows_s[g * EXPERTS_PER_GROUP + j], ws)
        within_b.append(wb)
        within_s.append(ws)

    def argmax_first(skip):
        bv, bi, bs = None, None, None
        for j in range(EXPERTS_PER_GROUP):
            v = within_b[j] if skip is None else jnp.where(skip == j, -jnp.inf, within_b[j])
            if j == 0:
                bv, bi, bs = v, jnp.zeros(v.shape, jnp.int32), within_s[0]
            else:
                upd = v > bv
                bv = jnp.where(upd, v, bv)
                bi = jnp.where(upd, j, bi)
                bs = jnp.where(upd, within_s[j], bs)
        return bi, bs

    i1, s1 = argmax_first(None)
    i2, s2 = argmax_first(i1)
    tot = s1 + s2
    eidx = jnp.concatenate([grp * EXPERTS_PER_GROUP + i1, grp * EXPERTS_PER_GROUP + i2], axis=0)
    wts = jnp.concatenate([s1 / tot, s2 / tot], axis=0)
    return eidx, wts


def _assignment_ranks(eidx, cnt_scr):
    tm = eidx.shape[1]
    expert = lax.broadcasted_iota(jnp.int32, (N_EXPERTS, tm), 0)
    onehot = jnp.concatenate([expert == eidx[0:1, :], expert == eidx[1:2, :]], axis=0).astype(F32)
    earlier = (lax.broadcasted_iota(jnp.int32, (tm, tm), 0)
               < lax.broadcasted_iota(jnp.int32, (tm, tm), 1)).astype(BF16)
    before = _dot(onehot.astype(BF16), earlier)
    total = jnp.sum(onehot, axis=1, keepdims=True)
    carried = cnt_scr[:, 0:1]
    first = onehot[:N_EXPERTS] * (carried + before[:N_EXPERTS])
    second = onehot[N_EXPERTS:] * (carried + total[:N_EXPERTS] + before[N_EXPERTS:])
    rank = jnp.concatenate([jnp.sum(first, axis=0, keepdims=True), jnp.sum(second, axis=0, keepdims=True)], axis=0)
    cnt_scr[...] = jnp.broadcast_to(carried + total[:N_EXPERTS] + total[N_EXPERTS:], cnt_scr.shape)
    return rank.astype(jnp.int32)


def _merge_kernel(x_ref, mod_ref, ya_ref, ybp_ref, ybs_ref, ycp_ref, ycs_ref, g_refs, wa_ref, wb_ref, wc_ref,
                  wo_ref, nw_ref, rw_ref, rb_ref, x1_ref, h2_ref, eidx_ref, wts_ref, rank_ref, cnt_ref, cnt_scr,
                  *, first_latent_tile):
    half = COL_BLK
    tile = pl.program_id(0)

    @pl.when(tile == 0)
    def _():
        cnt_scr[...] = jnp.zeros_like(cnt_scr)

    is_ctx = tile < first_latent_tile
    pa = _dot(ya_ref[...], wa_ref[...])
    pb = _dot(jnp.where(is_ctx, ybp_ref[...], ybs_ref[...]), wb_ref[...])
    pc = _dot(jnp.where(is_ctx, ycp_ref[...], ycs_ref[...]), wc_ref[...])
    parts = []
    for c in range(2):
        cols = slice(c * half, (c + 1) * half)
        m = (_sigmoid(g_refs[0 + c][...].astype(F32)) * pa[:, cols]
             + _sigmoid(g_refs[2 + c][...].astype(F32)) * pb[:, cols]
             + _sigmoid(g_refs[4 + c][...].astype(F32)) * pc[:, cols])
        parts.append(m.astype(BF16))
    merged = jnp.concatenate(parts, axis=1)
    x1 = x_ref[...] + mod_ref[0, 2:3, :] * _dot(merged, wo_ref[...])
    x1_ref[...] = x1
    h2 = _rms(x1) * nw_ref[...] * (1.0 + mod_ref[0, 4:5, :]) + mod_ref[0, 3:4, :]
    h2_ref[...] = h2
    hh, hl = _split_bf16(h2)
    rh, rl = _split_bf16(rw_ref[...])
    logits = (lax.dot_general(rh, hh, NT_DIMS, preferred_element_type=F32)
              + lax.dot_general(rh, hl, NT_DIMS, preferred_element_type=F32)
              + lax.dot_general(rl, hh, NT_DIMS, preferred_element_type=F32))
    s = _sigmoid(logits)
    eidx, wts = _route(s + rb_ref[...], s)
    eidx_ref[...] = eidx
    wts_ref[...] = wts
    rank_ref[...] = _assignment_ranks(eidx, cnt_scr)
    cnt_ref[...] = cnt_scr[...]


def _merge(x, mod, mod_row, ya, yb_ctx, yb_lat, yc_ctx, yc_lat, z, w_br_a, w_br_b, w_br_c, w_out, norm2_w,
           router_wt, router_b, tm):
    t, d = x.shape
    n_ctx = yb_ctx.shape[0] // tm
    last_ctx = n_ctx - 1
    const = lambda shape: pl.BlockSpec(shape, lambda i: (0,) * len(shape))
    ctx_rows = lambda i: (jnp.minimum(i, last_ctx), 0)
    lat_rows = lambda i: (jnp.maximum(i - n_ctx, 0), 0)
    g_specs = [pl.BlockSpec((tm, COL_BLK), functools.partial(lambda i, cb: (i, cb), cb=CB_G + n)) for n in range(6)]

    def body(x_ref, mod_ref, ya_ref, ybp_ref, ybs_ref, ycp_ref, ycs_ref, g0, g1, g2, g3, g4, g5, *rest):
        _merge_kernel(x_ref, mod_ref, ya_ref, ybp_ref, ybs_ref, ycp_ref, ycs_ref, (g0, g1, g2, g3, g4, g5), *rest,
                      first_latent_tile=n_ctx)

    token_major = pl.BlockSpec((TOP_K, tm), lambda i: (0, i))
    return pl.pallas_call(
        body,
        grid=(t // tm,),
        in_specs=[pl.BlockSpec((tm, d), lambda i: (i, 0)),
                  pl.BlockSpec((1, 6, d), lambda i: (mod_row(i), 0, 0)),
                  pl.BlockSpec((tm, A_WIDTH), lambda i: (i, 0)),
                  pl.BlockSpec((tm, B_WIDTH), ctx_rows),
                  pl.BlockSpec((tm, B_WIDTH), lat_rows),
                  pl.BlockSpec((tm, C_WIDTH), ctx_rows),
                  pl.BlockSpec((tm, C_WIDTH), lat_rows),
                  *g_specs,
                  const(w_br_a.shape), const(w_br_b.shape), const(w_br_c.shape), const(w_out.shape),
                  const((1, d)), const(router_wt.shape), const((N_EXPERTS, 1))],
        out_specs=[pl.BlockSpec((tm, d), lambda i: (i, 0)),
                   pl.BlockSpec((tm, d), lambda i: (i, 0)),
                   token_major, token_major, token_major,
                   const((N_EXPERTS, LANES))],
        out_shape=[jax.ShapeDtypeStruct((t, d), F32),
                   jax.ShapeDtypeStruct((t, d), F32),
                   jax.ShapeDtypeStruct((TOP_K, t), jnp.int32),
                   jax.ShapeDtypeStruct((TOP_K, t), F32),
                   jax.ShapeDtypeStruct((TOP_K, t), jnp.int32),
                   jax.ShapeDtypeStruct((N_EXPERTS, LANES), F32)],
        scratch_shapes=[pltpu.VMEM((N_EXPERTS, LANES), F32)],
        compiler_params=_params("arbitrary"),
        name="merge_router",
    )(x, mod, ya, yb_ctx, yb_lat, yc_ctx, yc_lat, z, z, z, z, z, z, w_br_a, w_br_b, w_br_c, w_out, norm2_w,
      router_wt, router_b)


def _expert_kernel(dest_ref, te_ref, nv_ref, nu_ref, h_hbm, wg_ref, wu_ref, wd_ref, y_ref, src_ref, hbuf, sem,
                   *, tm, n_tok):
    i = pl.program_id(0)
    n_used = nu_ref[0]
    groups = tm // SUBLANES

    def start_gather(tile, slot):
        base = tile * tm
        last_valid = nv_ref[tile] - 1

        def body(j, carry):
            for u in range(SUBLANES):
                tok = src_ref[base + jnp.minimum(j * SUBLANES + u, last_valid)]
                pltpu.make_async_copy(h_hbm.at[pl.ds(tok, 1), :], hbuf.at[slot * groups + j, pl.ds(u, 1), :],
                                      sem.at[slot]).start()
            return carry

        lax.fori_loop(0, groups, body, 0)

    @pl.when(i == 0)
    def _():
        def invert(t, carry):
            src_ref[dest_ref[t]] = t
            src_ref[dest_ref[n_tok + t]] = t
            return carry

        lax.fori_loop(0, n_tok, invert, 0, unroll=8)
        start_gather(0, 0)

    @pl.when(i + 1 < n_used)
    def _():
        start_gather(i + 1, (i + 1) % 2)

    @pl.when(i < n_used)
    def _():
        slot = i % 2
        rows = hbuf.at[pl.ds(slot * groups, groups)]
        pltpu.make_async_copy(rows, rows, sem.at[slot]).wait()
        h = rows[...].reshape(tm, rows.shape[-1]).astype(BF16)
        g = _dot(h, wg_ref[...].astype(BF16))
        a = (g * _sigmoid(g)) * _dot(h, wu_ref[...].astype(BF16))
        y_ref[...] = _dot(a.astype(BF16), wd_ref[...].astype(BF16)).astype(y_ref.dtype)

    @pl.when(i >= n_used)
    def _():
        y_ref[...] = jnp.zeros_like(y_ref)


def _experts(h2, dest, tile_expert, valid_rows, n_used, w_g, w_u, w_d, tm):
    n_tok, d = h2.shape
    de = w_g.shape[-1]
    n_tiles = tile_expert.shape[0]
    weight = lambda shape: pl.BlockSpec((None,) + shape, lambda i, dst, te, nv, nu: (te[i], 0, 0))
    grid_spec = pltpu.PrefetchScalarGridSpec(
        num_scalar_prefetch=4,
        grid=(n_tiles,),
        in_specs=[pl.BlockSpec(memory_space=pl.ANY), weight((d, de)), weight((d, de)), weight((de, d))],
        out_specs=pl.BlockSpec((tm, d), lambda i, dst, te, nv, nu: (i, 0)),
        scratch_shapes=[pltpu.SMEM((n_tiles * tm,), jnp.int32),
                        pltpu.VMEM((2 * tm // SUBLANES, SUBLANES, d), F32),
                        pltpu.SemaphoreType.DMA((2,))],
    )
    return pl.pallas_call(
        functools.partial(_expert_kernel, tm=tm, n_tok=n_tok),
        grid_spec=grid_spec,
        out_shape=jax.ShapeDtypeStruct((n_tiles * tm, d), BF16),
        compiler_params=_params("arbitrary"),
        name="experts",
    )(dest.reshape(TOP_K * n_tok), tile_expert, valid_rows, n_used, h2, w_g, w_u, w_d)


def _dispatch_plan(eidx, rank, counts, tm):
    t = eidx.shape[1]
    n_tiles = (TOP_K * t + N_EXPERTS * (tm - 1)) // tm
    tiles_per = (counts + tm - 1) // tm
    tile_end = jnp.cumsum(tiles_per)
    row_start = (tile_end - tiles_per) * tm
    experts = jnp.arange(N_EXPERTS, dtype=jnp.int32)
    start_of = jnp.sum(jnp.where(eidx[None] == experts[:, None, None], row_start[:, None, None], 0), axis=0)
    dest = rank + start_of
    tile_ids = jnp.arange(n_tiles, dtype=jnp.int32)
    tile_expert = jnp.minimum(jnp.sum((tile_end[None, :] <= tile_ids[:, None]).astype(jnp.int32), axis=1),
                              N_EXPERTS - 1)
    row_end = jnp.take(row_start + counts, tile_expert)
    valid_rows = jnp.clip(row_end - tile_ids * tm, 0, tm).astype(jnp.int32)
    n_used = tile_end[-1:].astype(jnp.int32)
    return dest, tile_expert.astype(jnp.int32), valid_rows, n_used


def _combine_kernel(x_ref, mod_ref, y0_ref, y1_ref, w_ref, o_ref):
    w = w_ref[...]
    y = w[:, 0:1] * y0_ref[...].astype(F32) + w[:, 1:2] * y1_ref[...].astype(F32)
    o_ref[...] = x_ref[...] + mod_ref[0, 5:6, :] * y


def _combine(x, mod, mod_row, y0, y1, w_cols, tm):
    t, d = x.shape
    return pl.pallas_call(
        _combine_kernel,
        grid=(t // tm,),
        in_specs=[pl.BlockSpec((tm, d), lambda i: (i, 0)),
                  pl.BlockSpec((1, 6, d), lambda i: (mod_row(i), 0, 0)),
                  pl.BlockSpec((tm, d), lambda i: (i, 0)),
                  pl.BlockSpec((tm, d), lambda i: (i, 0)),
                  pl.BlockSpec((tm, TOP_K), lambda i: (i, 0))],
        out_specs=pl.BlockSpec((tm, d), lambda i: (i, 0)),
        out_shape=jax.ShapeDtypeStruct((t, d), F32),
        compiler_params=_params("parallel"),
        name="combine",
    )(x, mod, y0, y1, w_cols)


def _pick_tile(preferred, *sizes):
    tile = preferred
    while any(s % tile for s in sizes):
        tile //= 2
    return tile


def kernel(x_prompt, x_sample, cache_k, cache_v, c, c_ctx, w_mod, b_mod, norm1_w, norm2_w, w_in, a_norm_w, a_w_s, a_b_s, q_norm_w, k_norm_w, lambda_qk, subln_w, w_br_a, w_br_b, w_br_c, w_out, router_w, router_b, w_e_gate, w_e_up, w_e_down):
    batch, seq, d = x_prompt.shape
    dec_batch, dec_seq, _ = x_sample.shape
    depth = w_mod.shape[0]
    past = cache_k.shape[2]
    tp, ts = batch * seq, dec_batch * dec_seq
    t = tp + ts
    assert tp % dec_seq == 0 and dec_seq % seq == 0 and seq % CHUNK == 0 and dec_seq % GRID_W == 0

    tm = _pick_tile(1024, tp, dec_seq)
    tm_small = _pick_tile(512, tp, dec_seq)
    tm_exp = 512
    tn_in = 1536

    def mod_row_for(tile):
        first_lat = tp // tile
        per_batch = dec_seq // tile
        return lambda i: jnp.where(i < first_lat, 0, 1 + (i - first_lat) // per_batch)

    n_rows = 1 + dec_batch
    r_pad = -(-n_rows // 8) * 8
    cvec = jnp.zeros((r_pad, d), F32).at[0].set(c_ctx).at[1:n_rows].set(c)
    mod_all = _mod_vectors(cvec, w_mod, b_mod)[:, :n_rows].reshape(depth, n_rows, 6, d)

    rope_tabs = _rope_tables(dec_seq)
    dft_p = _dft_tables(seq)
    dft_s = _dft_tables(dec_seq)
    cache_k2 = cache_k.reshape(dec_batch, depth, past, B_HEADS * 2 * B_QK_DIM)
    cache_v2 = cache_v.reshape(dec_batch, depth, past, B_WIDTH)
    router_wt = router_w.T
    router_b2 = router_b.reshape(N_EXPERTS, 1)

    x = jnp.concatenate([x_prompt.reshape(tp, d), x_sample.reshape(ts, d)], axis=0)
    new_k, new_v = [], []
    for l in range(depth):
        mod = mod_all[l]
        lam_init = 0.8 - 0.6 * math.exp(-0.3 * l)
        z = _in_proj(x, mod, mod_row_for(tm), norm1_w[l].reshape(1, d), w_in[l].astype(BF16), tm, tn_in)

        bias_a = jnp.repeat(a_b_s[l].T, A_GROUP_DIM, axis=1)
        ya = _mixer_a(z, a_norm_w[l].reshape(1, A_WIDTH), a_w_s[l].astype(BF16), bias_a, tm_small)

        qw = jnp.tile(q_norm_w[l], 2).reshape(1, LANES)
        kw = jnp.tile(k_norm_w[l], 2).reshape(1, LANES)
        sw = subln_w[l].reshape(1, LANES)
        tq_p = _pick_tile(512, tp)
        qp, kp, k32 = _qk_prep(z, 0, tp, qw, kw, None, True, tq_p)
        tq_s = _pick_tile(512, dec_seq)
        qs, ks = _qk_prep(z, tp, ts, qw, kw, rope_tabs, False, tq_s)
        yb_p = _attention(qp, kp, z, 0, batch, seq, lambda_qk[l], sw, lam_init, None, seq, B_HEADS)
        yb_s = _attention(qs, ks, z, tp, dec_batch, dec_seq, lambda_qk[l], sw, lam_init,
                          (cache_k2, cache_v2, l), _pick_tile(512, dec_seq), 1)

        yc_p = _fourier(z, 0, batch, seq, dft_p, seq)
        yc_s = _fourier(z, tp, dec_batch, dec_seq, dft_s, _pick_tile(512, dec_seq))

        x1, h2, eidx, wts, rank, cnt = _merge(
            x, mod, mod_row_for(tm_small), ya, yb_p, yb_s, yc_p, yc_s, z,
            w_br_a[l].astype(BF16), w_br_b[l].astype(BF16), w_br_c[l].astype(BF16), w_out[l].astype(BF16),
            norm2_w[l].reshape(1, d), router_wt, router_b2, tm_small)

        dest, tile_expert, valid_rows, n_used = _dispatch_plan(eidx, rank, cnt[:, 0].astype(jnp.int32), tm_exp)
        yg = _experts(h2, dest, tile_expert, valid_rows, n_used, w_e_gate[l], w_e_up[l], w_e_down[l], tm_exp)
        y0 = jnp.take(yg, dest[0], axis=0)
        y1 = jnp.take(yg, dest[1], axis=0)
        x = _combine(x1, mod, mod_row_for(tm_small), y0, y1, wts.T, tm_small)

        new_k.append(k32.reshape(batch, seq, B_HEADS, 2 * B_QK_DIM))
        v_cols = z[:tp, CB_VAL * COL_BLK:CB_VAL * COL_BLK + B_WIDTH]
        new_v.append(v_cols.astype(F32).reshape(batch, seq, B_HEADS, B_V_DIM))

    y_prompt = x[:tp].reshape(batch, seq, d)
    y_sample = x[tp:].reshape(dec_batch, dec_seq, d)
    return (y_prompt, y_sample, jnp.stack(new_k, axis=1), jnp.stack(new_v, axis=1))
```

```python
import functools
import math

import jax
import jax.numpy as jnp
from jax import lax
from jax.experimental import pallas as pl
from jax.experimental.pallas import tpu as pltpu

F32 = jnp.float32
BF16 = jnp.bfloat16

EPS = 1e-6
GRID_W = 64
CHUNK = 128
A_GROUPS = 8
A_GROUP_DIM = 64
A_WIDTH = A_GROUPS * A_GROUP_DIM
B_HEADS = 8
B_QK_DIM = 64
B_V_DIM = 2 * B_QK_DIM
B_WIDTH = B_HEADS * B_V_DIM
ROPE_THETA = 10000.0
ROPE_FREQS = B_QK_DIM // 4
C_GROUPS = 4
C_GROUP_DIM = 128
C_WIDTH = C_GROUPS * C_GROUP_DIM
N_EXPERTS = 16
N_EXPERT_GROUPS = 4
EXPERTS_PER_GROUP = N_EXPERTS // N_EXPERT_GROUPS
TOP_K = 2

COL_BLK = 512
CB_U, CB_V, CB_Q, CB_K, CB_VAL, CB_F, CB_G = 0, 1, 2, 4, 6, 8, 9

LOG2_E = 1.4426950408889634
LANES = 128
SUBLANES = 8
VMEM_LIMIT = 56 * 1024 * 1024

NT_DIMS = (((1,), (1,)), ((), ()))


def _params(*sem):
    return pltpu.CompilerParams(dimension_semantics=sem, vmem_limit_bytes=VMEM_LIMIT)


def _dot(a, b):
    return jnp.dot(a, b, preferred_element_type=F32)


def _split_bf16(a):
    hi = a.astype(BF16)
    lo = (a - hi.astype(F32)).astype(BF16)
    return hi, lo


def _gelu_tanh(x):
    return 0.5 * x * (1.0 + jnp.tanh(0.7978845608028654 * (x + 0.044715 * (x * x * x))))


def _sigmoid(x):
    return 1.0 / (1.0 + jnp.exp(-x))


def _rms(x):
    return x * lax.rsqrt(jnp.mean(x * x, axis=-1, keepdims=True) + EPS)


def _mod_kernel(c_ref, w_ref, b_ref, o_ref):
    c = c_ref[...]
    a = c * _sigmoid(c)
    ah, al = _split_bf16(a)
    wh, wl = _split_bf16(w_ref[0])
    o_ref[0] = _dot(ah, wh) + _dot(al, wh) + _dot(ah, wl) + b_ref[0]


def _mod_vectors(cvec, w_mod, b_mod):
    depth, d, n = w_mod.shape
    r = cvec.shape[0]
    tn = 1536
    return pl.pallas_call(
        _mod_kernel,
        grid=(depth, n // tn),
        in_specs=[pl.BlockSpec((r, d), lambda l, j: (0, 0)),
                  pl.BlockSpec((1, d, tn), lambda l, j: (l, 0, j)),
                  pl.BlockSpec((1, 1, tn), lambda l, j: (l, 0, j))],
        out_specs=pl.BlockSpec((1, r, tn), lambda l, j: (l, 0, j)),
        out_shape=jax.ShapeDtypeStruct((depth, r, n), F32),
        compiler_params=_params("parallel", "parallel"),
        name="mod_vectors",
    )(cvec, w_mod, b_mod.reshape(depth, 1, n))


def _in_proj_kernel(x_ref, mod_ref, nw_ref, w_ref, z_ref, h_scr):
    @pl.when(pl.program_id(1) == 0)
    def _():
        y = _rms(x_ref[...]) * nw_ref[...]
        h_scr[...] = (y * (1.0 + mod_ref[0, 1:2, :]) + mod_ref[0, 0:1, :]).astype(BF16)

    z_ref[...] = _dot(h_scr[...], w_ref[...]).astype(z_ref.dtype)


def _in_proj(x, mod, mod_row, norm_w, w_in, tm, tn):
    t, d = x.shape
    n = w_in.shape[1]
    return pl.pallas_call(
        _in_proj_kernel,
        grid=(t // tm, n // tn),
        in_specs=[pl.BlockSpec((tm, d), lambda i, j: (i, 0)),
                  pl.BlockSpec((1, 6, d), lambda i, j: (mod_row(i), 0, 0)),
                  pl.BlockSpec((1, d), lambda i, j: (0, 0)),
                  pl.BlockSpec((d, tn), lambda i, j: (0, j))],
        out_specs=pl.BlockSpec((tm, tn), lambda i, j: (i, j)),
        out_shape=jax.ShapeDtypeStruct((t, n), BF16),
        scratch_shapes=[pltpu.VMEM((tm, d), BF16)],
        compiler_params=_params("parallel", "arbitrary"),
        name="in_proj",
    )(x, mod, norm_w, w_in)


def _mixer_a_kernel(u_ref, v_ref, nw_ref, ws_ref, bias_ref, o_ref):
    tm = u_ref.shape[0]
    v = _rms(_gelu_tanh(v_ref[...].astype(F32))) * nw_ref[...]
    vb = v.astype(BF16)
    lo = lax.broadcasted_iota(jnp.int32, (1, LANES), 1) < A_GROUP_DIM
    for c in range(tm // CHUNK):
        rows = slice(c * CHUNK, (c + 1) * CHUNK)
        for j in range(A_WIDTH // LANES):
            cols = slice(j * LANES, (j + 1) * LANES)
            blk = vb[rows, cols]
            sv = jnp.where(lo, _dot(ws_ref[2 * j], blk), _dot(ws_ref[2 * j + 1], blk)) + bias_ref[:, cols]
            u = _gelu_tanh(u_ref[rows, cols].astype(F32))
            o_ref[rows, cols] = (u * sv).astype(o_ref.dtype)


def _mixer_a(z, norm_w, w_s, bias, tm):
    t = z.shape[0]
    return pl.pallas_call(
        _mixer_a_kernel,
        grid=(t // tm,),
        in_specs=[pl.BlockSpec((tm, A_WIDTH), lambda i: (i, CB_U)),
                  pl.BlockSpec((tm, A_WIDTH), lambda i: (i, CB_V)),
                  pl.BlockSpec((1, A_WIDTH), lambda i: (0, 0)),
                  pl.BlockSpec((A_GROUPS, CHUNK, CHUNK), lambda i: (0, 0, 0)),
                  pl.BlockSpec((CHUNK, A_WIDTH), lambda i: (0, 0))],
        out_specs=pl.BlockSpec((tm, A_WIDTH), lambda i: (i, 0)),
        out_shape=jax.ShapeDtypeStruct((t, A_WIDTH), BF16),
        compiler_params=_params("parallel"),
        name="mixer_a",
    )(z, z, norm_w, w_s, bias)


def _group_rms(x, lo):
    sq = x * x
    s_lo = jnp.sum(jnp.where(lo, sq, 0.0), axis=-1, keepdims=True)
    s_hi = jnp.sum(jnp.where(lo, 0.0, sq), axis=-1, keepdims=True)
    inv = jnp.where(lo, lax.rsqrt(s_lo * (1.0 / B_QK_DIM) + EPS), lax.rsqrt(s_hi * (1.0 / B_QK_DIM) + EPS))
    return x * inv


def _qk_prep_kernel(*refs, rope, emit_k32):
    zq_ref, zk_ref, qw_ref, kw_ref = refs[:4]
    pos = 4
    if rope:
        cos_ref, sin_ref = refs[pos:pos + 2]
        pos += 2
    q_ref, k_ref = refs[pos:pos + 2]
    k32_ref = refs[pos + 2] if emit_k32 else None

    lane = lax.broadcasted_iota(jnp.int32, (1, LANES), 1)
    lo = lane < B_QK_DIM
    first_half = (lane % (2 * ROPE_FREQS)) < ROPE_FREQS
    q_scale = LOG2_E / math.sqrt(B_QK_DIM)

    def prep(x, w):
        y = _group_rms(x.astype(F32), lo) * w
        if rope:
            swapped = jnp.where(first_half, pltpu.roll(y, LANES - ROPE_FREQS, axis=1),
                                pltpu.roll(y, ROPE_FREQS, axis=1))
            y = y * cos_ref[...] + swapped * sin_ref[...]
        return y

    for h in range(B_HEADS):
        cols = slice(h * LANES, (h + 1) * LANES)
        q = prep(zq_ref[:, cols], qw_ref[...])
        q_ref[:, cols] = (q * q_scale).astype(q_ref.dtype)
        k = prep(zk_ref[:, cols], kw_ref[...])
        k_ref[:, cols] = k.astype(k_ref.dtype)
        if emit_k32:
            k32_ref[:, cols] = k


def _qk_prep(z, row0, rows, qw, kw, rope_tabs, emit_k32, tm):
    rb0 = row0 // tm
    width = B_HEADS * LANES
    wq, wk = CB_Q * COL_BLK // width, CB_K * COL_BLK // width
    in_specs = [pl.BlockSpec((tm, width), lambda i: (rb0 + i, wq)),
                pl.BlockSpec((tm, width), lambda i: (rb0 + i, wk)),
                pl.BlockSpec((1, LANES), lambda i: (0, 0)),
                pl.BlockSpec((1, LANES), lambda i: (0, 0))]
    args = [z, z, qw, kw]
    if rope_tabs is not None:
        seq_blocks = rope_tabs[0].shape[0] // tm
        in_specs += [pl.BlockSpec((tm, LANES), lambda i: (i % seq_blocks, 0))] * 2
        args += list(rope_tabs)
    out_specs = [pl.BlockSpec((tm, width), lambda i: (i, 0))] * (3 if emit_k32 else 2)
    out_shape = [jax.ShapeDtypeStruct((rows, width), BF16)] * 2
    if emit_k32:
        out_shape.append(jax.ShapeDtypeStruct((rows, width), F32))
    return pl.pallas_call(
        functools.partial(_qk_prep_kernel, rope=rope_tabs is not None, emit_k32=emit_k32),
        grid=(rows // tm,),
        in_specs=in_specs,
        out_specs=out_specs,
        out_shape=out_shape,
        compiler_params=_params("parallel"),
        name="qk_prep",
    )(*args)


def _rope_tables(n_tokens):
    t = jnp.arange(n_tokens)
    row = (t // GRID_W).astype(F32)
    col = (t % GRID_W).astype(F32)
    freq = ROPE_THETA ** (-jnp.arange(ROPE_FREQS, dtype=F32) / ROPE_FREQS)
    ang_r = row[:, None] * freq[None, :]
    ang_c = col[:, None] * freq[None, :]
    cos = jnp.concatenate([jnp.cos(ang_r)] * 2 + [jnp.cos(ang_c)] * 2, axis=1)
    sin = jnp.concatenate([-jnp.sin(ang_r), jnp.sin(ang_r), -jnp.sin(ang_c), jnp.sin(ang_c)], axis=1)
    return jnp.tile(cos, (1, 2)), jnp.tile(sin, (1, 2))


def _attn_kernel(*refs, lam_init, has_ctx, heads):
    lq_ref, sw_ref, q_ref, k_ref, v_ref = refs[:5]
    if has_ctx:
        ck_ref, cv_ref, o_ref, vt_scr, cvt_scr = refs[5:]
    else:
        o_ref, vt_scr = refs[5:]

    @pl.when(pl.program_id(2) == 0)
    def _():
        vt_scr[...] = v_ref[...].astype(F32).T.astype(BF16)
        if has_ctx:
            cvt_scr[...] = cv_ref[...].T.astype(BF16)

    lq = lq_ref[...]
    s01 = jnp.sum(lq[0:1] * lq[1:2], axis=-1, keepdims=True)
    s23 = jnp.sum(lq[2:3] * lq[3:4], axis=-1, keepdims=True)
    lam = jnp.exp(s01) - jnp.exp(s23) + lam_init
    lo = lax.broadcasted_iota(jnp.int32, (1, LANES), 1) < B_QK_DIM

    for h in range(heads):
        cols = slice(h * LANES, (h + 1) * LANES)
        q = q_ref[:, cols]
        k = k_ref[:, cols]
        zero = jnp.zeros_like(q)
        outs = []
        for m in range(2):
            qm = jnp.where(lo, q, zero) if m == 0 else jnp.where(lo, zero, q)
            s_l = lax.dot_general(k, qm, NT_DIMS, preferred_element_type=F32)
            mx = jnp.max(s_l, axis=0, keepdims=True)
            if has_ctx:
                s_c = lax.dot_general(ck_ref[:, cols].astype(BF16), qm, NT_DIMS, preferred_element_type=F32)
                mx = jnp.maximum(mx, jnp.max(s_c, axis=0, keepdims=True))
            e_l = jnp.exp2(s_l - mx)
            den = jnp.sum(e_l, axis=0, keepdims=True)
            o_m = _dot(vt_scr[cols, :], e_l.astype(BF16))
            if has_ctx:
                e_c = jnp.exp2(s_c - mx)
                den = den + jnp.sum(e_c, axis=0, keepdims=True)
                o_m = o_m + _dot(cvt_scr[cols, :], e_c.astype(BF16))
            outs.append(o_m * ((1.0 if m == 0 else lam) / den))
        o_t = outs[0] - outs[1]
        o_t = o_t * lax.rsqrt(jnp.mean(o_t * o_t, axis=0, keepdims=True) + EPS)
        o_ref[:, cols] = (o_t.T * (sw_ref[...] * (1.0 - lam_init))).astype(o_ref.dtype)


def _attention(q, k, z, row0, batch, seq, lam_qk, subln_w, lam_init, ctx, tq, heads):
    nq = seq // tq
    width = heads * LANES
    vcol0 = CB_VAL * COL_BLK // width
    vrow0 = row0 // seq
    in_specs = [pl.BlockSpec((4, B_QK_DIM), lambda b, h, i: (0, 0)),
                pl.BlockSpec((1, LANES), lambda b, h, i: (0, 0)),
                pl.BlockSpec((tq, width), lambda b, h, i: (b * nq + i, h)),
                pl.BlockSpec((seq, width), lambda b, h, i: (b, h)),
                pl.BlockSpec((seq, width), lambda b, h, i: (vrow0 + b, vcol0 + h))]
    args = [lam_qk, subln_w, q, k, z]
    scratch = [pltpu.VMEM((width, seq), BF16)]
    if ctx is not None:
        cache_k, cache_v, layer = ctx
        past = cache_k.shape[2]
        spec = pl.BlockSpec((None, None, past, width), lambda b, h, i: (b, layer, 0, h))
        in_specs += [spec, spec]
        args += [cache_k, cache_v]
        scratch.append(pltpu.VMEM((width, past), BF16))
    return pl.pallas_call(
        functools.partial(_attn_kernel, lam_init=lam_init, has_ctx=ctx is not None, heads=heads),
        grid=(batch, B_HEADS // heads, nq),
        in_specs=in_specs,
        out_specs=pl.BlockSpec((tq, width), lambda b, h, i: (b * nq + i, h)),
        out_shape=jax.ShapeDtypeStruct((batch * seq, B_WIDTH), BF16),
        scratch_shapes=scratch,
        compiler_params=_params("parallel", "parallel", "arbitrary"),
        name="diff_attention",
    )(*args)


def _dft_tables(seq):
    def cos_sin(n):
        idx = jnp.arange(n, dtype=jnp.int32)
        ang = ((idx[:, None] * idx[None, :]) % n).astype(F32) * (2.0 * math.pi / n)
        s = 1.0 / math.sqrt(n)
        return jnp.cos(ang) * s, jnp.sin(ang) * s

    c_l, s_l = cos_sin(seq)
    c_c, s_c = cos_sin(C_GROUP_DIM)
    eye = jnp.eye(C_GROUPS, dtype=F32)
    chan = jnp.concatenate([jnp.kron(eye, c_c), jnp.kron(eye, s_c)], axis=1)
    return jnp.concatenate([c_l, -s_l], axis=1).astype(BF16), chan.astype(BF16)


def _fourier_kernel(f_ref, chan_ref, w_ref, o_ref, xcs_scr):
    seq = f_ref.shape[0]

    @pl.when(pl.program_id(1) == 0)
    def _():
        t = _dot(f_ref[...], chan_ref[...])
        xcs_scr[0:seq, :] = t[:, :C_WIDTH].astype(BF16)
        xcs_scr[seq:2 * seq, :] = t[:, C_WIDTH:].astype(BF16)

    o_ref[...] = _dot(w_ref[...], xcs_scr[...]).astype(o_ref.dtype)


def _fourier(z, row0, batch, seq, tables, tr):
    w, chan = tables
    nr = seq // tr
    rb0 = row0 // seq
    return pl.pallas_call(
        _fourier_kernel,
        grid=(batch, nr),
        in_specs=[pl.BlockSpec((seq, C_WIDTH), lambda b, r: (rb0 + b, CB_F)),
                  pl.BlockSpec((C_WIDTH, 2 * C_WIDTH), lambda b, r: (0, 0)),
                  pl.BlockSpec((tr, 2 * seq), lambda b, r: (r, 0))],
        out_specs=pl.BlockSpec((tr, C_WIDTH), lambda b, r: (b * nr + r, 0)),
        out_shape=jax.ShapeDtypeStruct((batch * seq, C_WIDTH), BF16),
        scratch_shapes=[pltpu.VMEM((2 * seq, C_WIDTH), BF16)],
        compiler_params=_params("parallel", "arbitrary"),
        name="fourier_mix",
    )(z, chan, w)


def _route(sb, s):
    rows_b = [sb[e:e + 1, :] for e in range(N_EXPERTS)]
    rows_s = [s[e:e + 1, :] for e in range(N_EXPERTS)]
    best, grp = None, None
    for g in range(N_EXPERT_GROUPS):
        r = rows_b[g * EXPERTS_PER_GROUP:(g + 1) * EXPERTS_PER_GROUP]
        pair = None
        for a in range(EXPERTS_PER_GROUP):
            for b in range(a + 1, EXPERTS_PER_GROUP):
                pair = r[a] + r[b] if pair is None else jnp.maximum(pair, r[a] + r[b])
        if g == 0:
            best, grp = pair, jnp.zeros(pair.shape, jnp.int32)
        else:
            upd = pair > best
            grp = jnp.where(upd, g, grp)
            best = jnp.where(upd, pair, best)
    within_b, within_s = [], []
    for j in range(EXPERTS_PER_GROUP):
        wb, ws = rows_b[j], rows_s[j]
        for g in range(1, N_EXPERT_GROUPS):
            wb = jnp.where(grp == g, rows_b[g * EXPERTS_PER_GROUP + j], wb)
            ws = jnp.where(grp == g, rows_s[g * EXPERTS_PER_GROUP + j], ws)
        within_b.append(wb)
        within_s.append(ws)

    def argmax_first(skip):
        bv, bi, bs = None, None, None
        for j in range(EXPERTS_PER_GROUP):
            v = within_b[j] if skip is None else jnp.where(skip == j, -jnp.inf, within_b[j])
            if j == 0:
                bv, bi, bs = v, jnp.zeros(v.shape, jnp.int32), within_s[0]
            else:
                upd = v > bv
                bv = jnp.where(upd, v, bv)
                bi = jnp.where(upd, j, bi)
                bs = jnp.where(upd, within_s[j], bs)
        return bi, bs

    i1, s1 = argmax_first(None)
    i2, s2 = argmax_first(i1)
    tot = s1 + s2
    eidx = jnp.concatenate([grp * EXPERTS_PER_GROUP + i1, grp * EXPERTS_PER_GROUP + i2], axis=0)
    wts = jnp.concatenate([s1 / tot, s2 / tot], axis=0)
    return eidx, wts


def _assignment_ranks(eidx, cnt_scr):
    tm = eidx.shape[1]
    expert = lax.broadcasted_iota(jnp.int32, (N_EXPERTS, tm), 0)
    onehot = jnp.concatenate([expert == eidx[0:1, :], expert == eidx[1:2, :]], axis=0).astype(F32)
    earlier = (lax.broadcasted_iota(jnp.int32, (tm, tm), 0)
               < lax.broadcasted_iota(jnp.int32, (tm, tm), 1)).astype(BF16)
    before = _dot(onehot.astype(BF16), earlier)
    total = jnp.sum(onehot, axis=1, keepdims=True)
    carried = cnt_scr[:, 0:1]
    first = onehot[:N_EXPERTS] * (carried + before[:N_EXPERTS])
    second = onehot[N_EXPERTS:] * (carried + total[:N_EXPERTS] + before[N_EXPERTS:])
    rank = jnp.concatenate([jnp.sum(first, axis=0, keepdims=True), jnp.sum(second, axis=0, keepdims=True)], axis=0)
    cnt_scr[...] = jnp.broadcast_to(carried + total[:N_EXPERTS] + total[N_EXPERTS:], cnt_scr.shape)
    return rank.astype(jnp.int32)


def _merge_kernel(x_ref, mod_ref, ya_ref, ybp_ref, ybs_ref, ycp_ref, ycs_ref, g_refs, wa_ref, wb_ref, wc_ref,
                  wo_ref, nw_ref, rw_ref, rb_ref, x1_ref, h2_ref, eidx_ref, wts_ref, rank_ref, cnt_ref, cnt_scr,
                  *, first_latent_tile):
    half = COL_BLK
    tile = pl.program_id(0)

    @pl.when(tile == 0)
    def _():
        cnt_scr[...] = jnp.zeros_like(cnt_scr)

    is_ctx = tile < first_latent_tile
    pa = _dot(ya_ref[...], wa_ref[...])
    pb = _dot(jnp.where(is_ctx, ybp_ref[...], ybs_ref[...]), wb_ref[...])
    pc = _dot(jnp.where(is_ctx, ycp_ref[...], ycs_ref[...]), wc_ref[...])
    parts = []
    for c in range(2):
        cols = slice(c * half, (c + 1) * half)
        m = (_sigmoid(g_refs[0 + c][...].astype(F32)) * pa[:, cols]
             + _sigmoid(g_refs[2 + c][...].astype(F32)) * pb[:, cols]
             + _sigmoid(g_refs[4 + c][...].astype(F32)) * pc[:, cols])
        parts.append(m.astype(BF16))
    merged = jnp.concatenate(parts, axis=1)
    x1 = x_ref[...] + mod_ref[0, 2:3, :] * _dot(merged, wo_ref[...])
    x1_ref[...] = x1
    h2 = _rms(x1) * nw_ref[...] * (1.0 + mod_ref[0, 4:5, :]) + mod_ref[0, 3:4, :]
    for c in range(SUBLANES):
        h2_ref[:, c, :] = h2[:, c * LANES:(c + 1) * LANES]
    hh, hl = _split_bf16(h2)
    rh, rl = _split_bf16(rw_ref[...])
    logits = (lax.dot_general(rh, hh, NT_DIMS, preferred_element_type=F32)
              + lax.dot_general(rh, hl, NT_DIMS, preferred_element_type=F32)
              + lax.dot_general(rl, hh, NT_DIMS, preferred_element_type=F32))
    s = _sigmoid(logits)
    eidx, wts = _route(s + rb_ref[...], s)
    eidx_ref[...] = eidx
    wts_ref[...] = wts
    rank_ref[...] = _assignment_ranks(eidx, cnt_scr)
    cnt_ref[...] = cnt_scr[...]


def _merge(x, mod, mod_row, ya, yb_ctx, yb_lat, yc_ctx, yc_lat, z, w_br_a, w_br_b, w_br_c, w_out, norm2_w,
           router_wt, router_b, tm):
    t, d = x.shape
    n_ctx = yb_ctx.shape[0] // tm
    last_ctx = n_ctx - 1
    const = lambda shape: pl.BlockSpec(shape, lambda i: (0,) * len(shape))
    ctx_rows = lambda i: (jnp.minimum(i, last_ctx), 0)
    lat_rows = lambda i: (jnp.maximum(i - n_ctx, 0), 0)
    g_specs = [pl.BlockSpec((tm, COL_BLK), functools.partial(lambda i, cb: (i, cb), cb=CB_G + n)) for n in range(6)]

    def body(x_ref, mod_ref, ya_ref, ybp_ref, ybs_ref, ycp_ref, ycs_ref, g0, g1, g2, g3, g4, g5, *rest):
        _merge_kernel(x_ref, mod_ref, ya_ref, ybp_ref, ybs_ref, ycp_ref, ycs_ref, (g0, g1, g2, g3, g4, g5), *rest,
                      first_latent_tile=n_ctx)

    token_major = pl.BlockSpec((TOP_K, tm), lambda i: (0, i))
    return pl.pallas_call(
        body,
        grid=(t // tm,),
        in_specs=[pl.BlockSpec((tm, d), lambda i: (i, 0)),
                  pl.BlockSpec((1, 6, d), lambda i: (mod_row(i), 0, 0)),
                  pl.BlockSpec((tm, A_WIDTH), lambda i: (i, 0)),
                  pl.BlockSpec((tm, B_WIDTH), ctx_rows),
                  pl.BlockSpec((tm, B_WIDTH), lat_rows),
                  pl.BlockSpec((tm, C_WIDTH), ctx_rows),
                  pl.BlockSpec((tm, C_WIDTH), lat_rows),
                  *g_specs,
                  const(w_br_a.shape), const(w_br_b.shape), const(w_br_c.shape), const(w_out.shape),
                  const((1, d)), const(router_wt.shape), const((N_EXPERTS, 1))],
        out_specs=[pl.BlockSpec((tm, d), lambda i: (i, 0)),
                   pl.BlockSpec((tm, SUBLANES, LANES), lambda i: (i, 0, 0)),
                   token_major, token_major, token_major,
                   const((N_EXPERTS, LANES))],
        out_shape=[jax.ShapeDtypeStruct((t, d), F32),
                   jax.ShapeDtypeStruct((t, SUBLANES, LANES), F32),
                   jax.ShapeDtypeStruct((TOP_K, t), jnp.int32),
                   jax.ShapeDtypeStruct((TOP_K, t), F32),
                   jax.ShapeDtypeStruct((TOP_K, t), jnp.int32),
                   jax.ShapeDtypeStruct((N_EXPERTS, LANES), F32)],
        scratch_shapes=[pltpu.VMEM((N_EXPERTS, LANES), F32)],
        compiler_params=_params("arbitrary"),
        name="merge_router",
    )(x, mod, ya, yb_ctx, yb_lat, yc_ctx, yc_lat, z, z, z, z, z, z, w_br_a, w_br_b, w_br_c, w_out, norm2_w,
      router_wt, router_b)


def _expert_kernel(dest_ref, te_ref, nv_ref, nu_ref, h_hbm, wg_ref, wu_ref, wd_ref, y_ref, src_ref, hbuf0, hbuf1, sem,
                   *, tm, n_tok):
    i = pl.program_id(0)
    n_used = nu_ref[0]
    bufs = (hbuf0, hbuf1)

    def row_copy(tile, r, slot):
        base = tile * tm
        last_valid = nv_ref[tile] - 1
        tok = src_ref[base + jnp.minimum(r, last_valid)]
        dst = bufs[slot].at[r // SUBLANES, :, r % SUBLANES, :]
        return pltpu.make_async_copy(h_hbm.at[tok], dst, sem.at[slot])

    def wait_rows(slot):
        pltpu.make_async_copy(bufs[slot], bufs[slot], sem.at[slot]).wait()

    def dense_rows(slot):
        blocks = [bufs[slot][:, c].reshape(tm, LANES) for c in range(SUBLANES)]
        return jnp.concatenate(blocks, axis=1)

    @pl.when(i == 0)
    def _():
        def invert(t, carry):
            src_ref[dest_ref[t]] = t
            src_ref[dest_ref[n_tok + t]] = t
            return carry

        lax.fori_loop(0, n_tok, invert, 0, unroll=8)
        for r in range(tm):
            row_copy(0, r, 0).start()

    for slot in range(2):
        @pl.when(jnp.logical_and(i < n_used, i % 2 == slot))
        def _():
            wait_rows(slot)
            h = dense_rows(slot).astype(BF16)
            nxt = jnp.minimum(i + 1, n_used - 1)
            for r in range(tm):
                row_copy(nxt, r, 1 - slot).start()
            g = _dot(h, wg_ref[...].astype(BF16))
            a = (g * _sigmoid(g)) * _dot(h, wu_ref[...].astype(BF16))
            y_ref[...] = _dot(a.astype(BF16), wd_ref[...].astype(BF16)).astype(y_ref.dtype)

            @pl.when(i == n_used - 1)
            def _():
                wait_rows(1 - slot)

    @pl.when(i >= n_used)
    def _():
        y_ref[...] = jnp.zeros_like(y_ref)


def _experts(h2, dest, tile_expert, valid_rows, n_used, w_g, w_u, w_d, layer, tm):
    n_tok = h2.shape[0]
    d, de = w_g.shape[-2:]
    assert d == SUBLANES * LANES
    n_tiles = tile_expert.shape[0]
    weight = lambda shape: pl.BlockSpec((None, None) + shape, lambda i, dst, te, nv, nu: (layer, te[i], 0, 0))
    grid_spec = pltpu.PrefetchScalarGridSpec(
        num_scalar_prefetch=4,
        grid=(n_tiles,),
        in_specs=[pl.BlockSpec(memory_space=pl.ANY), weight((d, de)), weight((d, de)), weight((de, d))],
        out_specs=pl.BlockSpec((tm, d), lambda i, dst, te, nv, nu: (i, 0)),
        scratch_shapes=[pltpu.SMEM((n_tiles * tm,), jnp.int32),
                        pltpu.VMEM((tm // SUBLANES, SUBLANES, SUBLANES, LANES), F32),
                        pltpu.VMEM((tm // SUBLANES, SUBLANES, SUBLANES, LANES), F32),
                        pltpu.SemaphoreType.DMA((2,))],
    )
    return pl.pallas_call(
        functools.partial(_expert_kernel, tm=tm, n_tok=n_tok),
        grid_spec=grid_spec,
        out_shape=jax.ShapeDtypeStruct((n_tiles * tm, d), BF16),
        compiler_params=_params("arbitrary"),
        name="experts",
    )(dest.reshape(TOP_K * n_tok), tile_expert, valid_rows, n_used, h2, w_g, w_u, w_d)


def _dispatch_plan(eidx, rank, counts, tm):
    t = eidx.shape[1]
    n_tiles = (TOP_K * t + N_EXPERTS * (tm - 1)) // tm
    tiles_per = (counts + tm - 1) // tm
    tile_end = jnp.cumsum(tiles_per)
    row_start = (tile_end - tiles_per) * tm
    experts = jnp.arange(N_EXPERTS, dtype=jnp.int32)
    start_of = jnp.sum(jnp.where(eidx[None] == experts[:, None, None], row_start[:, None, None], 0), axis=0)
    dest = rank + start_of
    tile_ids = jnp.arange(n_tiles, dtype=jnp.int32)
    tile_expert = jnp.minimum(jnp.sum((tile_end[None, :] <= tile_ids[:, None]).astype(jnp.int32), axis=1),
                              N_EXPERTS - 1)
    row_end = jnp.take(row_start + counts, tile_expert)
    valid_rows = jnp.clip(row_end - tile_ids * tm, 0, tm).astype(jnp.int32)
    n_used = tile_end[-1:].astype(jnp.int32)
    return dest, tile_expert.astype(jnp.int32), valid_rows, n_used


def _combine_kernel(x_ref, mod_ref, y0_ref, y1_ref, w_ref, o_ref):
    w = w_ref[...]
    y = w[:, 0:1] * y0_ref[...].astype(F32) + w[:, 1:2] * y1_ref[...].astype(F32)
    o_ref[...] = x_ref[...] + mod_ref[0, 5:6, :] * y


def _combine(x, mod, mod_row, y0, y1, w_cols, tm, row0, rows):
    d = x.shape[1]
    rb0 = row0 // tm
    token_rows = lambda width: pl.BlockSpec((tm, width), lambda i: (rb0 + i, 0))
    return pl.pallas_call(
        _combine_kernel,
        grid=(rows // tm,),
        in_specs=[token_rows(d),
                  pl.BlockSpec((1, 6, d), lambda i: (mod_row(rb0 + i), 0, 0)),
                  token_rows(d), token_rows(d), token_rows(TOP_K)],
        out_specs=pl.BlockSpec((tm, d), lambda i: (i, 0)),
        out_shape=jax.ShapeDtypeStruct((rows, d), F32),
        compiler_params=_params("parallel"),
        name="combine",
    )(x, mod, y0, y1, w_cols)


def _pick_tile(preferred, *sizes):
    tile = preferred
    while any(s % tile for s in sizes):
        tile //= 2
    return tile


def kernel(x_prompt, x_sample, cache_k, cache_v, c, c_ctx, w_mod, b_mod, norm1_w, norm2_w, w_in, a_norm_w, a_w_s, a_b_s, q_norm_w, k_norm_w, lambda_qk, subln_w, w_br_a, w_br_b, w_br_c, w_out, router_w, router_b, w_e_gate, w_e_up, w_e_down):
    batch, seq, d = x_prompt.shape
    dec_batch, dec_seq, _ = x_sample.shape
    depth = w_mod.shape[0]
    past = cache_k.shape[2]
    tp, ts = batch * seq, dec_batch * dec_seq
    t = tp + ts
    assert tp % dec_seq == 0 and dec_seq % seq == 0 and seq % CHUNK == 0 and dec_seq % GRID_W == 0

    tm = _pick_tile(1024, tp, dec_seq)
    tm_small = _pick_tile(512, tp, dec_seq)
    tm_exp = 512
    tn_in = 1536

    def mod_row_for(tile):
        first_lat = tp // tile
        per_batch = dec_seq // tile
        return lambda i: jnp.where(i < first_lat, 0, 1 + (i - first_lat) // per_batch)

    n_rows = 1 + dec_batch
    r_pad = -(-n_rows // 8) * 8
    cvec = jnp.zeros((r_pad, d), F32).at[0].set(c_ctx).at[1:n_rows].set(c)
    mod_all = _mod_vectors(cvec, w_mod, b_mod)[:, :n_rows].reshape(depth, n_rows, 6, d)

    rope_tabs = _rope_tables(dec_seq)
    dft_p = _dft_tables(seq)
    dft_s = _dft_tables(dec_seq)
    cache_k2 = cache_k.reshape(dec_batch, depth, past, B_HEADS * 2 * B_QK_DIM)
    cache_v2 = cache_v.reshape(dec_batch, depth, past, B_WIDTH)
    router_wt = router_w.T
    router_b2 = router_b.reshape(N_EXPERTS, 1)

    x = jnp.concatenate([x_prompt.reshape(tp, d), x_sample.reshape(ts, d)], axis=0)
    new_k, new_v = [], []
    for l in range(depth):
        mod = mod_all[l]
        lam_init = 0.8 - 0.6 * math.exp(-0.3 * l)
        z = _in_proj(x, mod, mod_row_for(tm), norm1_w[l].reshape(1, d), w_in[l].astype(BF16), tm, tn_in)

        bias_a = jnp.repeat(a_b_s[l].T, A_GROUP_DIM, axis=1)
        ya = _mixer_a(z, a_norm_w[l].reshape(1, A_WIDTH), a_w_s[l].astype(BF16), bias_a, tm_small)

        qw = jnp.tile(q_norm_w[l], 2).reshape(1, LANES)
        kw = jnp.tile(k_norm_w[l], 2).reshape(1, LANES)
        sw = subln_w[l].reshape(1, LANES)
        tq_p = _pick_tile(512, tp)
        qp, kp, k32 = _qk_prep(z, 0, tp, qw, kw, None, True, tq_p)
        tq_s = _pick_tile(512, dec_seq)
        qs, ks = _qk_prep(z, tp, ts, qw, kw, rope_tabs, False, tq_s)
        yb_p = _attention(qp, kp, z, 0, batch, seq, lambda_qk[l], sw, lam_init, None, seq, B_HEADS)
        yb_s = _attention(qs, ks, z, tp, dec_batch, dec_seq, lambda_qk[l], sw, lam_init,
                          (cache_k2, cache_v2, l), _pick_tile(512, dec_seq), 1)

        yc_p = _fourier(z, 0, batch, seq, dft_p, seq)
        yc_s = _fourier(z, tp, dec_batch, dec_seq, dft_s, _pick_tile(512, dec_seq))

        x1, h2, eidx, wts, rank, cnt = _merge(
            x, mod, mod_row_for(tm_small), ya, yb_p, yb_s, yc_p, yc_s, z,
            w_br_a[l].astype(BF16), w_br_b[l].astype(BF16), w_br_c[l].astype(BF16), w_out[l].astype(BF16),
            norm2_w[l].reshape(1, d), router_wt, router_b2, tm_small)

        dest, tile_expert, valid_rows, n_used = _dispatch_plan(eidx, rank, cnt[:, 0].astype(jnp.int32), tm_exp)
        yg = _experts(h2, dest, tile_expert, valid_rows, n_used, w_e_gate, w_e_up, w_e_down, l, tm_exp)
        y0 = yg.at[dest[0]].get(mode="promise_in_bounds")
        y1 = yg.at[dest[1]].get(mode="promise_in_bounds")
        combine = functools.partial(_combine, x1, mod, mod_row_for(tm_small), y0, y1, wts.T, tm_small)
        if l + 1 < depth:
            x = combine(0, t)
        else:
            y_prompt = combine(0, tp).reshape(batch, seq, d)
            y_sample = combine(tp, ts).reshape(dec_batch, dec_seq, d)

        new_k.append(k32.reshape(batch, seq, B_HEADS, 2 * B_QK_DIM))
        v_cols = z[:tp, CB_VAL * COL_BLK:CB_VAL * COL_BLK + B_WIDTH]
        new_v.append(v_cols.astype(F32).reshape(batch, seq, B_HEADS, B_V_DIM))

    return (y_prompt, y_sample, jnp.stack(new_k, axis=1), jnp.stack(new_v, axis=1))
```

```python
import functools
import math

import jax
import jax.numpy as jnp
from jax import lax
from jax.experimental import pallas as pl
from jax.experimental.pallas import tpu as pltpu

F32 = jnp.float32
BF16 = jnp.bfloat16

EPS = 1e-6
GRID_W = 64
CHUNK = 128
A_GROUPS = 8
A_GROUP_DIM = 64
A_WIDTH = A_GROUPS * A_GROUP_DIM
B_HEADS = 8
B_QK_DIM = 64
B_V_DIM = 2 * B_QK_DIM
B_WIDTH = B_HEADS * B_V_DIM
ROPE_THETA = 10000.0
ROPE_FREQS = B_QK_DIM // 4
C_GROUPS = 4
C_GROUP_DIM = 128
C_WIDTH = C_GROUPS * C_GROUP_DIM
N_EXPERTS = 16
N_EXPERT_GROUPS = 4
EXPERTS_PER_GROUP = N_EXPERTS // N_EXPERT_GROUPS
TOP_K = 2

COL_BLK = 512
CB_U, CB_V, CB_Q, CB_K, CB_VAL, CB_F, CB_G = 0, 1, 2, 4, 6, 8, 9

LOG2_E = 1.4426950408889634
LANES = 128
SUBLANES = 8
VMEM_LIMIT = 56 * 1024 * 1024

NT_DIMS = (((1,), (1,)), ((), ()))


def _params(*sem):
    return pltpu.CompilerParams(dimension_semantics=sem, vmem_limit_bytes=VMEM_LIMIT)


def _dot(a, b):
    return jnp.dot(a, b, preferred_element_type=F32)


def _split_bf16(a):
    hi = a.astype(BF16)
    lo = (a - hi.astype(F32)).astype(BF16)
    return hi, lo


def _gelu_tanh(x):
    return 0.5 * x * (1.0 + jnp.tanh(0.7978845608028654 * (x + 0.044715 * (x * x * x))))


def _sigmoid(x):
    return 1.0 / (1.0 + jnp.exp(-x))


def _rms(x):
    return x * lax.rsqrt(jnp.mean(x * x, axis=-1, keepdims=True) + EPS)


def _mod_kernel(c_ref, w_ref, b_ref, o_ref):
    c = c_ref[...]
    a = c * _sigmoid(c)
    ah, al = _split_bf16(a)
    wh, wl = _split_bf16(w_ref[0])
    o_ref[0] = _dot(ah, wh) + _dot(al, wh) + _dot(ah, wl) + b_ref[0]


def _mod_vectors(cvec, w_mod, b_mod):
    depth, d, n = w_mod.shape
    r = cvec.shape[0]
    tn = 1536
    return pl.pallas_call(
        _mod_kernel,
        grid=(depth, n // tn),
        in_specs=[pl.BlockSpec((r, d), lambda l, j: (0, 0)),
                  pl.BlockSpec((1, d, tn), lambda l, j: (l, 0, j)),
                  pl.BlockSpec((1, 1, tn), lambda l, j: (l, 0, j))],
        out_specs=pl.BlockSpec((1, r, tn), lambda l, j: (l, 0, j)),
        out_shape=jax.ShapeDtypeStruct((depth, r, n), F32),
        compiler_params=_params("parallel", "parallel"),
        name="mod_vectors",
    )(cvec, w_mod, b_mod.reshape(depth, 1, n))


def _in_proj_kernel(x_ref, mod_ref, nw_ref, w_ref, z_ref, h_scr):
    @pl.when(pl.program_id(1) == 0)
    def _():
        y = _rms(x_ref[...]) * nw_ref[...]
        h_scr[...] = (y * (1.0 + mod_ref[0, 1:2, :]) + mod_ref[0, 0:1, :]).astype(BF16)

    z_ref[...] = _dot(h_scr[...], w_ref[...]).astype(z_ref.dtype)


def _in_proj(x, mod, mod_row, norm_w, w_in, tm, tn):
    t, d = x.shape
    n = w_in.shape[1]
    return pl.pallas_call(
        _in_proj_kernel,
        grid=(t // tm, n // tn),
        in_specs=[pl.BlockSpec((tm, d), lambda i, j: (i, 0)),
                  pl.BlockSpec((1, 6, d), lambda i, j: (mod_row(i), 0, 0)),
                  pl.BlockSpec((1, d), lambda i, j: (0, 0)),
                  pl.BlockSpec((d, tn), lambda i, j: (0, j))],
        out_specs=pl.BlockSpec((tm, tn), lambda i, j: (i, j)),
        out_shape=jax.ShapeDtypeStruct((t, n), BF16),
        scratch_shapes=[pltpu.VMEM((tm, d), BF16)],
        compiler_params=_params("parallel", "arbitrary"),
        name="in_proj",
    )(x, mod, norm_w, w_in)


def _mixer_a_kernel(u_ref, v_ref, nw_ref, ws_ref, bias_ref, o_ref):
    tm = u_ref.shape[0]
    v = _rms(_gelu_tanh(v_ref[...].astype(F32))) * nw_ref[...]
    vb = v.astype(BF16)
    lo = lax.broadcasted_iota(jnp.int32, (1, LANES), 1) < A_GROUP_DIM
    for c in range(tm // CHUNK):
        rows = slice(c * CHUNK, (c + 1) * CHUNK)
        for j in range(A_WIDTH // LANES):
            cols = slice(j * LANES, (j + 1) * LANES)
            blk = vb[rows, cols]
            sv = jnp.where(lo, _dot(ws_ref[2 * j], blk), _dot(ws_ref[2 * j + 1], blk)) + bias_ref[:, cols]
            u = _gelu_tanh(u_ref[rows, cols].astype(F32))
            o_ref[rows, cols] = (u * sv).astype(o_ref.dtype)


def _mixer_a(z, norm_w, w_s, bias, tm):
    t = z.shape[0]
    return pl.pallas_call(
        _mixer_a_kernel,
        grid=(t // tm,),
        in_specs=[pl.BlockSpec((tm, A_WIDTH), lambda i: (i, CB_U)),
                  pl.BlockSpec((tm, A_WIDTH), lambda i: (i, CB_V)),
                  pl.BlockSpec((1, A_WIDTH), lambda i: (0, 0)),
                  pl.BlockSpec((A_GROUPS, CHUNK, CHUNK), lambda i: (0, 0, 0)),
                  pl.BlockSpec((CHUNK, A_WIDTH), lambda i: (0, 0))],
        out_specs=pl.BlockSpec((tm, A_WIDTH), lambda i: (i, 0)),
        out_shape=jax.ShapeDtypeStruct((t, A_WIDTH), BF16),
        compiler_params=_params("parallel"),
        name="mixer_a",
    )(z, z, norm_w, w_s, bias)


def _group_rms(x, lo):
    sq = x * x
    s_lo = jnp.sum(jnp.where(lo, sq, 0.0), axis=-1, keepdims=True)
    s_hi = jnp.sum(jnp.where(lo, 0.0, sq), axis=-1, keepdims=True)
    inv = jnp.where(lo, lax.rsqrt(s_lo * (1.0 / B_QK_DIM) + EPS), lax.rsqrt(s_hi * (1.0 / B_QK_DIM) + EPS))
    return x * inv


def _qk_prep_kernel(*refs, rope, emit_k32):
    zq_ref, zk_ref, qw_ref, kw_ref = refs[:4]
    pos = 4
    if rope:
        cos_ref, sin_ref = refs[pos:pos + 2]
        pos += 2
    q_ref, k_ref = refs[pos:pos + 2]
    k32_ref = refs[pos + 2] if emit_k32 else None

    lane = lax.broadcasted_iota(jnp.int32, (1, LANES), 1)
    lo = lane < B_QK_DIM
    first_half = (lane % (2 * ROPE_FREQS)) < ROPE_FREQS
    q_scale = LOG2_E / math.sqrt(B_QK_DIM)

    def prep(x, w):
        y = _group_rms(x.astype(F32), lo) * w
        if rope:
            swapped = jnp.where(first_half, pltpu.roll(y, LANES - ROPE_FREQS, axis=1),
                                pltpu.roll(y, ROPE_FREQS, axis=1))
            y = y * cos_ref[...] + swapped * sin_ref[...]
        return y

    for h in range(B_HEADS):
        cols = slice(h * LANES, (h + 1) * LANES)
        q = prep(zq_ref[:, cols], qw_ref[...])
        q_ref[:, cols] = (q * q_scale).astype(q_ref.dtype)
        k = prep(zk_ref[:, cols], kw_ref[...])
        k_ref[:, cols] = k.astype(k_ref.dtype)
        if emit_k32:
            k32_ref[:, cols] = k


def _qk_prep(z, row0, rows, qw, kw, rope_tabs, emit_k32, tm):
    rb0 = row0 // tm
    width = B_HEADS * LANES
    wq, wk = CB_Q * COL_BLK // width, CB_K * COL_BLK // width
    in_specs = [pl.BlockSpec((tm, width), lambda i: (rb0 + i, wq)),
                pl.BlockSpec((tm, width), lambda i: (rb0 + i, wk)),
                pl.BlockSpec((1, LANES), lambda i: (0, 0)),
                pl.BlockSpec((1, LANES), lambda i: (0, 0))]
    args = [z, z, qw, kw]
    if rope_tabs is not None:
        seq_blocks = rope_tabs[0].shape[0] // tm
        in_specs += [pl.BlockSpec((tm, LANES), lambda i: (i % seq_blocks, 0))] * 2
        args += list(rope_tabs)
    out_specs = [pl.BlockSpec((tm, width), lambda i: (i, 0))] * (3 if emit_k32 else 2)
    out_shape = [jax.ShapeDtypeStruct((rows, width), BF16)] * 2
    if emit_k32:
        out_shape.append(jax.ShapeDtypeStruct((rows, width), F32))
    return pl.pallas_call(
        functools.partial(_qk_prep_kernel, rope=rope_tabs is not None, emit_k32=emit_k32),
        grid=(rows // tm,),
        in_specs=in_specs,
        out_specs=out_specs,
        out_shape=out_shape,
        compiler_params=_params("parallel"),
        name="qk_prep",
    )(*args)


def _rope_tables(n_tokens):
    t = jnp.arange(n_tokens)
    row = (t // GRID_W).astype(F32)
    col = (t % GRID_W).astype(F32)
    freq = ROPE_THETA ** (-jnp.arange(ROPE_FREQS, dtype=F32) / ROPE_FREQS)
    ang_r = row[:, None] * freq[None, :]
    ang_c = col[:, None] * freq[None, :]
    cos = jnp.concatenate([jnp.cos(ang_r)] * 2 + [jnp.cos(ang_c)] * 2, axis=1)
    sin = jnp.concatenate([-jnp.sin(ang_r), jnp.sin(ang_r), -jnp.sin(ang_c), jnp.sin(ang_c)], axis=1)
    return jnp.tile(cos, (1, 2)), jnp.tile(sin, (1, 2))


def _attn_kernel(*refs, lam_init, has_ctx, heads):
    lq_ref, sw_ref, q_ref, k_ref, v_ref = refs[:5]
    if has_ctx:
        ck_ref, cv_ref, o_ref, vt_scr, cvt_scr = refs[5:]
    else:
        o_ref, vt_scr = refs[5:]

    @pl.when(pl.program_id(2) == 0)
    def _():
        vt_scr[...] = v_ref[...].astype(F32).T.astype(BF16)
        if has_ctx:
            cvt_scr[...] = cv_ref[...].T.astype(BF16)

    lq = lq_ref[...]
    s01 = jnp.sum(lq[0:1] * lq[1:2], axis=-1, keepdims=True)
    s23 = jnp.sum(lq[2:3] * lq[3:4], axis=-1, keepdims=True)
    lam = jnp.exp(s01) - jnp.exp(s23) + lam_init
    lo = lax.broadcasted_iota(jnp.int32, (1, LANES), 1) < B_QK_DIM

    def scores(h, m):
        cols = slice(h * LANES, (h + 1) * LANES)
        q = q_ref[:, cols]
        zero = jnp.zeros_like(q)
        qm = jnp.where(lo, q, zero) if m == 0 else jnp.where(lo, zero, q)
        s_l = lax.dot_general(k_ref[:, cols], qm, NT_DIMS, preferred_element_type=F32)
        s_c = None
        if has_ctx:
            s_c = lax.dot_general(ck_ref[:, cols].astype(BF16), qm, NT_DIMS, preferred_element_type=F32)
        return s_l, s_c

    def weighted_values(h, m, s_l, s_c):
        cols = slice(h * LANES, (h + 1) * LANES)
        mx = jnp.max(s_l, axis=0, keepdims=True)
        if has_ctx:
            mx = jnp.maximum(mx, jnp.max(s_c, axis=0, keepdims=True))
        e_l = jnp.exp2(s_l - mx)
        den = jnp.sum(e_l, axis=0, keepdims=True)
        o_m = _dot(vt_scr[cols, :], e_l.astype(BF16))
        if has_ctx:
            e_c = jnp.exp2(s_c - mx)
            den = den + jnp.sum(e_c, axis=0, keepdims=True)
            o_m = o_m + _dot(cvt_scr[cols, :], e_c.astype(BF16))
        return o_m * ((1.0 if m == 0 else lam) / den)

    chains = [(h, m) for h in range(heads) for m in range(2)]
    pending = scores(*chains[0])
    outs = []
    for n, (h, m) in enumerate(chains):
        following = scores(*chains[n + 1]) if n + 1 < len(chains) else None
        outs.append(weighted_values(h, m, *pending))
        pending = following
        if m == 1:
            cols = slice(h * LANES, (h + 1) * LANES)
            o_t = outs[0] - outs[1]
            outs = []
            o_t = o_t * lax.rsqrt(jnp.mean(o_t * o_t, axis=0, keepdims=True) + EPS)
            o_ref[:, cols] = (o_t.T * (sw_ref[...] * (1.0 - lam_init))).astype(o_ref.dtype)


def _attention(q, k, z, row0, batch, seq, lam_qk, subln_w, lam_init, ctx, tq, heads):
    nq = seq // tq
    width = heads * LANES
    vcol0 = CB_VAL * COL_BLK // width
    vrow0 = row0 // seq
    in_specs = [pl.BlockSpec((4, B_QK_DIM), lambda b, h, i: (0, 0)),
                pl.BlockSpec((1, LANES), lambda b, h, i: (0, 0)),
                pl.BlockSpec((tq, width), lambda b, h, i: (b * nq + i, h)),
                pl.BlockSpec((seq, width), lambda b, h, i: (b, h)),
                pl.BlockSpec((seq, width), lambda b, h, i: (vrow0 + b, vcol0 + h))]
    args = [lam_qk, subln_w, q, k, z]
    scratch = [pltpu.VMEM((width, seq), BF16)]
    if ctx is not None:
        cache_k, cache_v, layer = ctx
        past = cache_k.shape[2]
        spec = pl.BlockSpec((None, None, past, width), lambda b, h, i: (b, layer, 0, h))
        in_specs += [spec, spec]
        args += [cache_k, cache_v]
        scratch.append(pltpu.VMEM((width, past), BF16))
    return pl.pallas_call(
        functools.partial(_attn_kernel, lam_init=lam_init, has_ctx=ctx is not None, heads=heads),
        grid=(batch, B_HEADS // heads, nq),
        in_specs=in_specs,
        out_specs=pl.BlockSpec((tq, width), lambda b, h, i: (b * nq + i, h)),
        out_shape=jax.ShapeDtypeStruct((batch * seq, B_WIDTH), BF16),
        scratch_shapes=scratch,
        compiler_params=_params("parallel", "parallel", "arbitrary"),
        name="diff_attention",
    )(*args)


def _dft_tables(seq):
    def cos_sin(n):
        idx = jnp.arange(n, dtype=jnp.int32)
        ang = ((idx[:, None] * idx[None, :]) % n).astype(F32) * (2.0 * math.pi / n)
        s = 1.0 / math.sqrt(n)
        return jnp.cos(ang) * s, jnp.sin(ang) * s

    c_l, s_l = cos_sin(seq)
    c_c, s_c = cos_sin(C_GROUP_DIM)
    eye = jnp.eye(C_GROUPS, dtype=F32)
    chan = jnp.concatenate([jnp.kron(eye, c_c), jnp.kron(eye, s_c)], axis=1)
    return jnp.concatenate([c_l, -s_l], axis=1).astype(BF16), chan.astype(BF16)


def _fourier_kernel(f_ref, chan_ref, w_ref, o_ref, xcs_scr):
    seq = f_ref.shape[0]

    @pl.when(pl.program_id(1) == 0)
    def _():
        t = _dot(f_ref[...], chan_ref[...])
        xcs_scr[0:seq, :] = t[:, :C_WIDTH].astype(BF16)
        xcs_scr[seq:2 * seq, :] = t[:, C_WIDTH:].astype(BF16)

    o_ref[...] = _dot(w_ref[...], xcs_scr[...]).astype(o_ref.dtype)


def _fourier(z, row0, batch, seq, tables, tr):
    w, chan = tables
    nr = seq // tr
    rb0 = row0 // seq
    return pl.pallas_call(
        _fourier_kernel,
        grid=(batch, nr),
        in_specs=[pl.BlockSpec((seq, C_WIDTH), lambda b, r: (rb0 + b, CB_F)),
                  pl.BlockSpec((C_WIDTH, 2 * C_WIDTH), lambda b, r: (0, 0)),
                  pl.BlockSpec((tr, 2 * seq), lambda b, r: (r, 0))],
        out_specs=pl.BlockSpec((tr, C_WIDTH), lambda b, r: (b * nr + r, 0)),
        out_shape=jax.ShapeDtypeStruct((batch * seq, C_WIDTH), BF16),
        scratch_shapes=[pltpu.VMEM((2 * seq, C_WIDTH), BF16)],
        compiler_params=_params("parallel", "arbitrary"),
        name="fourier_mix",
    )(z, chan, w)


def _route(sb, s):
    rows_b = [sb[e:e + 1, :] for e in range(N_EXPERTS)]
    rows_s = [s[e:e + 1, :] for e in range(N_EXPERTS)]
    best, grp = None, None
    for g in range(N_EXPERT_GROUPS):
        r = rows_b[g * EXPERTS_PER_GROUP:(g + 1) * EXPERTS_PER_GROUP]
        pair = None
        for a in range(EXPERTS_PER_GROUP):
            for b in range(a + 1, EXPERTS_PER_GROUP):
                pair = r[a] + r[b] if pair is None else jnp.maximum(pair, r[a] + r[b])
        if g == 0:
            best, grp = pair, jnp.zeros(pair.shape, jnp.int32)
        else:
            upd = pair > best
            grp = jnp.where(upd, g, grp)
            best = jnp.where(upd, pair, best)
    within_b, within_s = [], []
    for j in range(EXPERTS_PER_GROUP):
        wb, ws = rows_b[j], rows_s[j]
        for g in range(1, N_EXPERT_GROUPS):
            wb = jnp.where(grp == g, rows_b[g * EXPERTS_PER_GROUP + j], wb)
            ws = jnp.where(grp == g, rows_s[g * EXPERTS_PER_GROUP + j], ws)
        within_b.append(wb)
        within_s.append(ws)

    def argmax_first(skip):
        bv, bi, bs = None, None, None
        for j in range(EXPERTS_PER_GROUP):
            v = within_b[j] if skip is None else jnp.where(skip == j, -jnp.inf, within_b[j])
            if j == 0:
                bv, bi, bs = v, jnp.zeros(v.shape, jnp.int32), within_s[0]
            else:
                upd = v > bv
                bv = jnp.where(upd, v, bv)
                bi = jnp.where(upd, j, bi)
                bs = jnp.where(upd, within_s[j], bs)
        return bi, bs

    i1, s1 = argmax_first(None)
    i2, s2 = argmax_first(i1)
    tot = s1 + s2
    eidx = jnp.concatenate([grp * EXPERTS_PER_GROUP + i1, grp * EXPERTS_PER_GROUP + i2], axis=0)
    wts = jnp.concatenate([s1 / tot, s2 / tot], axis=0)
    return eidx, wts


def _assignment_ranks(eidx, cnt_scr):
    tm = eidx.shape[1]
    expert = lax.broadcasted_iota(jnp.int32, (N_EXPERTS, tm), 0)
    onehot = jnp.concatenate([expert == eidx[0:1, :], expert == eidx[1:2, :]], axis=0).astype(F32)
    earlier = (lax.broadcasted_iota(jnp.int32, (tm, tm), 0)
               < lax.broadcasted_iota(jnp.int32, (tm, tm), 1)).astype(BF16)
    before = _dot(onehot.astype(BF16), earlier)
    total = jnp.sum(onehot, axis=1, keepdims=True)
    carried = cnt_scr[:, 0:1]
    first = onehot[:N_EXPERTS] * (carried + before[:N_EXPERTS])
    second = onehot[N_EXPERTS:] * (carried + total[:N_EXPERTS] + before[N_EXPERTS:])
    rank = jnp.concatenate([jnp.sum(first, axis=0, keepdims=True), jnp.sum(second, axis=0, keepdims=True)], axis=0)
    cnt_scr[...] = jnp.broadcast_to(carried + total[:N_EXPERTS] + total[N_EXPERTS:], cnt_scr.shape)
    return rank.astype(jnp.int32)


def _merge_kernel(x_ref, mod_ref, ya_ref, ybp_ref, ybs_ref, ycp_ref, ycs_ref, g_refs, wa_ref, wb_ref, wc_ref,
                  wo_ref, nw_ref, rw_ref, rb_ref, x1_ref, h2_ref, eidx_ref, wts_ref, rank_ref, cnt_ref, cnt_scr,
                  *, first_latent_tile):
    half = COL_BLK
    tile = pl.program_id(0)

    @pl.when(tile == 0)
    def _():
        cnt_scr[...] = jnp.zeros_like(cnt_scr)

    is_ctx = tile < first_latent_tile
    pa = _dot(ya_ref[...], wa_ref[...])
    pb = _dot(jnp.where(is_ctx, ybp_ref[...], ybs_ref[...]), wb_ref[...])
    pc = _dot(jnp.where(is_ctx, ycp_ref[...], ycs_ref[...]), wc_ref[...])
    parts = []
    for c in range(2):
        cols = slice(c * half, (c + 1) * half)
        m = (_sigmoid(g_refs[0 + c][...].astype(F32)) * pa[:, cols]
             + _sigmoid(g_refs[2 + c][...].astype(F32)) * pb[:, cols]
             + _sigmoid(g_refs[4 + c][...].astype(F32)) * pc[:, cols])
        parts.append(m.astype(BF16))
    merged = jnp.concatenate(parts, axis=1)
    x1 = x_ref[...] + mod_ref[0, 2:3, :] * _dot(merged, wo_ref[...])
    x1_ref[...] = x1
    h2 = _rms(x1) * nw_ref[...] * (1.0 + mod_ref[0, 4:5, :]) + mod_ref[0, 3:4, :]
    for c in range(SUBLANES):
        h2_ref[:, c, :] = h2[:, c * LANES:(c + 1) * LANES]
    hh, hl = _split_bf16(h2)
    rh, rl = _split_bf16(rw_ref[...])
    logits = (lax.dot_general(rh, hh, NT_DIMS, preferred_element_type=F32)
              + lax.dot_general(rh, hl, NT_DIMS, preferred_element_type=F32)
              + lax.dot_general(rl, hh, NT_DIMS, preferred_element_type=F32))
    s = _sigmoid(logits)
    eidx, wts = _route(s + rb_ref[...], s)
    eidx_ref[...] = eidx
    wts_ref[...] = wts
    rank_ref[...] = _assignment_ranks(eidx, cnt_scr)
    cnt_ref[...] = cnt_scr[...]


def _merge(x, mod, mod_row, ya, yb_ctx, yb_lat, yc_ctx, yc_lat, z, w_br_a, w_br_b, w_br_c, w_out, norm2_w,
           router_wt, router_b, tm):
    t, d = x.shape
    n_ctx = yb_ctx.shape[0] // tm
    last_ctx = n_ctx - 1
    const = lambda shape: pl.BlockSpec(shape, lambda i: (0,) * len(shape))
    ctx_rows = lambda i: (jnp.minimum(i, last_ctx), 0)
    lat_rows = lambda i: (jnp.maximum(i - n_ctx, 0), 0)
    g_specs = [pl.BlockSpec((tm, COL_BLK), functools.partial(lambda i, cb: (i, cb), cb=CB_G + n)) for n in range(6)]

    def body(x_ref, mod_ref, ya_ref, ybp_ref, ybs_ref, ycp_ref, ycs_ref, g0, g1, g2, g3, g4, g5, *rest):
        _merge_kernel(x_ref, mod_ref, ya_ref, ybp_ref, ybs_ref, ycp_ref, ycs_ref, (g0, g1, g2, g3, g4, g5), *rest,
                      first_latent_tile=n_ctx)

    token_major = pl.BlockSpec((TOP_K, tm), lambda i: (0, i))
    return pl.pallas_call(
        body,
        grid=(t // tm,),
        in_specs=[pl.BlockSpec((tm, d), lambda i: (i, 0)),
                  pl.BlockSpec((1, 6, d), lambda i: (mod_row(i), 0, 0)),
                  pl.BlockSpec((tm, A_WIDTH), lambda i: (i, 0)),
                  pl.BlockSpec((tm, B_WIDTH), ctx_rows),
                  pl.BlockSpec((tm, B_WIDTH), lat_rows),
                  pl.BlockSpec((tm, C_WIDTH), ctx_rows),
                  pl.BlockSpec((tm, C_WIDTH), lat_rows),
                  *g_specs,
                  const(w_br_a.shape), const(w_br_b.shape), const(w_br_c.shape), const(w_out.shape),
                  const((1, d)), const(router_wt.shape), const((N_EXPERTS, 1))],
        out_specs=[pl.BlockSpec((tm, d), lambda i: (i, 0)),
                   pl.BlockSpec((tm, SUBLANES, LANES), lambda i: (i, 0, 0)),
                   token_major, token_major, token_major,
                   const((N_EXPERTS, LANES))],
        out_shape=[jax.ShapeDtypeStruct((t, d), F32),
                   jax.ShapeDtypeStruct((t, SUBLANES, LANES), F32),
                   jax.ShapeDtypeStruct((TOP_K, t), jnp.int32),
                   jax.ShapeDtypeStruct((TOP_K, t), F32),
                   jax.ShapeDtypeStruct((TOP_K, t), jnp.int32),
                   jax.ShapeDtypeStruct((N_EXPERTS, LANES), F32)],
        scratch_shapes=[pltpu.VMEM((N_EXPERTS, LANES), F32)],
        compiler_params=_params("arbitrary"),
        name="merge_router",
    )(x, mod, ya, yb_ctx, yb_lat, yc_ctx, yc_lat, z, z, z, z, z, z, w_br_a, w_br_b, w_br_c, w_out, norm2_w,
      router_wt, router_b)


def _expert_kernel(dest_ref, te_ref, nv_ref, nu_ref, h_hbm, wg_ref, wu_ref, wd_ref, y_ref, src_ref, hbuf0, hbuf1, sem,
                   *, tm, n_tok):
    i = pl.program_id(0)
    n_used = nu_ref[0]
    bufs = (hbuf0, hbuf1)

    def row_copy(tile, r, slot):
        base = tile * tm
        last_valid = nv_ref[tile] - 1
        tok = src_ref[base + jnp.minimum(r, last_valid)]
        return pltpu.make_async_copy(h_hbm.at[tok], bufs[slot].at[r], sem.at[slot])

    def wait_rows(slot):
        pltpu.make_async_copy(bufs[slot], bufs[slot], sem.at[slot]).wait()

    def dense_rows(slot):
        return jnp.concatenate([bufs[slot][:, c, :] for c in range(SUBLANES)], axis=1)

    @pl.when(i == 0)
    def _():
        def invert(t, carry):
            src_ref[dest_ref[t]] = t
            src_ref[dest_ref[n_tok + t]] = t
            return carry

        lax.fori_loop(0, n_tok, invert, 0, unroll=8)
        for r in range(tm):
            row_copy(0, r, 0).start()

    for slot in range(2):
        @pl.when(jnp.logical_and(i < n_used, i % 2 == slot))
        def _():
            wait_rows(slot)
            h = dense_rows(slot).astype(BF16)
            nxt = jnp.minimum(i + 1, n_used - 1)
            for r in range(tm):
                row_copy(nxt, r, 1 - slot).start()
            g = _dot(h, wg_ref[...].astype(BF16))
            a = (g * _sigmoid(g)) * _dot(h, wu_ref[...].astype(BF16))
            y_ref[...] = _dot(a.astype(BF16), wd_ref[...].astype(BF16)).astype(y_ref.dtype)

            @pl.when(i == n_used - 1)
            def _():
                wait_rows(1 - slot)

    @pl.when(i >= n_used)
    def _():
        y_ref[...] = jnp.zeros_like(y_ref)


def _experts(h2, dest, tile_expert, valid_rows, n_used, w_g, w_u, w_d, layer, tm):
    n_tok = h2.shape[0]
    d, de = w_g.shape[-2:]
    assert d == SUBLANES * LANES
    n_tiles = tile_expert.shape[0]
    weight = lambda shape: pl.BlockSpec((None, None) + shape, lambda i, dst, te, nv, nu: (layer, te[i], 0, 0))
    grid_spec = pltpu.PrefetchScalarGridSpec(
        num_scalar_prefetch=4,
        grid=(n_tiles,),
        in_specs=[pl.BlockSpec(memory_space=pl.ANY), weight((d, de)), weight((d, de)), weight((de, d))],
        out_specs=pl.BlockSpec((tm, d), lambda i, dst, te, nv, nu: (i, 0)),
        scratch_shapes=[pltpu.SMEM((n_tiles * tm,), jnp.int32),
                        pltpu.VMEM((tm, SUBLANES, LANES), F32), pltpu.VMEM((tm, SUBLANES, LANES), F32),
                        pltpu.SemaphoreType.DMA((2,))],
    )
    return pl.pallas_call(
        functools.partial(_expert_kernel, tm=tm, n_tok=n_tok),
        grid_spec=grid_spec,
        out_shape=jax.ShapeDtypeStruct((n_tiles * tm, d), BF16),
        compiler_params=_params("arbitrary"),
        name="experts",
    )(dest.reshape(TOP_K * n_tok), tile_expert, valid_rows, n_used, h2, w_g, w_u, w_d)


def _dispatch_plan(eidx, rank, counts, tm):
    t = eidx.shape[1]
    n_tiles = (TOP_K * t + N_EXPERTS * (tm - 1)) // tm
    tiles_per = (counts + tm - 1) // tm
    tile_end = jnp.cumsum(tiles_per)
    row_start = (tile_end - tiles_per) * tm
    experts = jnp.arange(N_EXPERTS, dtype=jnp.int32)
    start_of = jnp.sum(jnp.where(eidx[None] == experts[:, None, None], row_start[:, None, None], 0), axis=0)
    dest = rank + start_of
    tile_ids = jnp.arange(n_tiles, dtype=jnp.int32)
    tile_expert = jnp.minimum(jnp.sum((tile_end[None, :] <= tile_ids[:, None]).astype(jnp.int32), axis=1),
                              N_EXPERTS - 1)
    row_end = jnp.take(row_start + counts, tile_expert)
    valid_rows = jnp.clip(row_end - tile_ids * tm, 0, tm).astype(jnp.int32)
    n_used = tile_end[-1:].astype(jnp.int32)
    return dest, tile_expert.astype(jnp.int32), valid_rows, n_used


def _combine_kernel(x_ref, mod_ref, y0_ref, y1_ref, w_ref, o_ref):
    w = w_ref[...]
    y = w[:, 0:1] * y0_ref[...].astype(F32) + w[:, 1:2] * y1_ref[...].astype(F32)
    o_ref[...] = x_ref[...] + mod_ref[0, 5:6, :] * y


def _combine(x, mod, mod_row, y0, y1, w_cols, tm, row0, rows):
    d = x.shape[1]
    rb0 = row0 // tm
    token_rows = lambda width: pl.BlockSpec((tm, width), lambda i: (rb0 + i, 0))
    return pl.pallas_call(
        _combine_kernel,
        grid=(rows // tm,),
        in_specs=[token_rows(d),
                  pl.BlockSpec((1, 6, d), lambda i: (mod_row(rb0 + i), 0, 0)),
                  token_rows(d), token_rows(d), token_rows(TOP_K)],
        out_specs=pl.BlockSpec((tm, d), lambda i: (i, 0)),
        out_shape=jax.ShapeDtypeStruct((rows, d), F32),
        compiler_params=_params("parallel"),
        name="combine",
    )(x, mod, y0, y1, w_cols)


def _pick_tile(preferred, *sizes):
    tile = preferred
    while any(s % tile for s in sizes):
        tile //= 2
    return tile


def kernel(x_prompt, x_sample, cache_k, cache_v, c, c_ctx, w_mod, b_mod, norm1_w, norm2_w, w_in, a_norm_w, a_w_s, a_b_s, q_norm_w, k_norm_w, lambda_qk, subln_w, w_br_a, w_br_b, w_br_c, w_out, router_w, router_b, w_e_gate, w_e_up, w_e_down):
    batch, seq, d = x_prompt.shape
    dec_batch, dec_seq, _ = x_sample.shape
    depth = w_mod.shape[0]
    past = cache_k.shape[2]
    tp, ts = batch * seq, dec_batch * dec_seq
    t = tp + ts
    assert tp % dec_seq == 0 and dec_seq % seq == 0 and seq % CHUNK == 0 and dec_seq % GRID_W == 0

    tm = _pick_tile(1024, tp, dec_seq)
    tm_small = _pick_tile(512, tp, dec_seq)
    tm_exp = 512
    tn_in = 1536

    def mod_row_for(tile):
        first_lat = tp // tile
        per_batch = dec_seq // tile
        return lambda i: jnp.where(i < first_lat, 0, 1 + (i - first_lat) // per_batch)

    n_rows = 1 + dec_batch
    r_pad = -(-n_rows // 8) * 8
    cvec = jnp.zeros((r_pad, d), F32).at[0].set(c_ctx).at[1:n_rows].set(c)
    mod_all = _mod_vectors(cvec, w_mod, b_mod)[:, :n_rows].reshape(depth, n_rows, 6, d)

    rope_tabs = _rope_tables(dec_seq)
    dft_p = _dft_tables(seq)
    dft_s = _dft_tables(dec_seq)
    cache_k2 = cache_k.reshape(dec_batch, depth, past, B_HEADS * 2 * B_QK_DIM)
    cache_v2 = cache_v.reshape(dec_batch, depth, past, B_WIDTH)
    router_wt = router_w.T
    router_b2 = router_b.reshape(N_EXPERTS, 1)

    x = jnp.concatenate([x_prompt.reshape(tp, d), x_sample.reshape(ts, d)], axis=0)
    new_k, new_v = [], []
    for l in range(depth):
        mod = mod_all[l]
        lam_init = 0.8 - 0.6 * math.exp(-0.3 * l)
        z = _in_proj(x, mod, mod_row_for(tm), norm1_w[l].reshape(1, d), w_in[l].astype(BF16), tm, tn_in)

        bias_a = jnp.repeat(a_b_s[l].T, A_GROUP_DIM, axis=1)
        ya = _mixer_a(z, a_norm_w[l].reshape(1, A_WIDTH), a_w_s[l].astype(BF16), bias_a, tm_small)

        qw = jnp.tile(q_norm_w[l], 2).reshape(1, LANES)
        kw = jnp.tile(k_norm_w[l], 2).reshape(1, LANES)
        sw = subln_w[l].reshape(1, LANES)
        tq_p = _pick_tile(512, tp)
        qp, kp, k32 = _qk_prep(z, 0, tp, qw, kw, None, True, tq_p)
        tq_s = _pick_tile(512, dec_seq)
        qs, ks = _qk_prep(z, tp, ts, qw, kw, rope_tabs, False, tq_s)
        yb_p = _attention(qp, kp, z, 0, batch, seq, lambda_qk[l], sw, lam_init, None, seq, B_HEADS)
        yb_s = _attention(qs, ks, z, tp, dec_batch, dec_seq, lambda_qk[l], sw, lam_init,
                          (cache_k2, cache_v2, l), _pick_tile(512, dec_seq), 1)

        yc_p = _fourier(z, 0, batch, seq, dft_p, seq)
        yc_s = _fourier(z, tp, dec_batch, dec_seq, dft_s, _pick_tile(512, dec_seq))

        x1, h2, eidx, wts, rank, cnt = _merge(
            x, mod, mod_row_for(tm_small), ya, yb_p, yb_s, yc_p, yc_s, z,
            w_br_a[l].astype(BF16), w_br_b[l].astype(BF16), w_br_c[l].astype(BF16), w_out[l].astype(BF16),
            norm2_w[l].reshape(1, d), router_wt, router_b2, tm_small)

        dest, tile_expert, valid_rows, n_used = _dispatch_plan(eidx, rank, cnt[:, 0].astype(jnp.int32), tm_exp)
        yg = _experts(h2, dest, tile_expert, valid_rows, n_used, w_e_gate, w_e_up, w_e_down, l, tm_exp)
        y0 = yg.at[dest[0]].get(mode="promise_in_bounds")
        y1 = yg.at[dest[1]].get(mode="promise_in_bounds")
        combine = functools.partial(_combine, x1, mod, mod_row_for(tm_small), y0, y1, wts.T, tm_small)
        if l + 1 < depth:
            x = combine(0, t)
        else:
            y_prompt = combine(0, tp).reshape(batch, seq, d)
            y_sample = combine(tp, ts).reshape(dec_batch, dec_seq, d)

        new_k.append(k32.reshape(batch, seq, B_HEADS, 2 * B_QK_DIM))
        v_cols = z[:tp, CB_VAL * COL_BLK:CB_VAL * COL_BLK + B_WIDTH]
        new_v.append(v_cols.astype(F32).reshape(batch, seq, B_HEADS, B_V_DIM))

    return (y_prompt, y_sample, jnp.stack(new_k, axis=1), jnp.stack(new_v, axis=1))
```

```python
import functools
import math

import jax
import jax.numpy as jnp
from jax import lax
from jax.experimental import pallas as pl
from jax.experimental.pallas import tpu as pltpu

F32 = jnp.float32
BF16 = jnp.bfloat16

EPS = 1e-6
GRID_W = 64
CHUNK = 128
A_GROUPS = 8
A_GROUP_DIM = 64
A_WIDTH = A_GROUPS * A_GROUP_DIM
B_HEADS = 8
B_QK_DIM = 64
B_V_DIM = 2 * B_QK_DIM
B_WIDTH = B_HEADS * B_V_DIM
ROPE_THETA = 10000.0
ROPE_FREQS = B_QK_DIM // 4
C_GROUPS = 4
C_GROUP_DIM = 128
C_WIDTH = C_GROUPS * C_GROUP_DIM
N_EXPERTS = 16
N_EXPERT_GROUPS = 4
EXPERTS_PER_GROUP = N_EXPERTS // N_EXPERT_GROUPS
TOP_K = 2

COL_BLK = 512
CB_U, CB_V, CB_Q, CB_K, CB_VAL, CB_F, CB_G = 0, 1, 2, 4, 6, 8, 9

LOG2_E = 1.4426950408889634
LANES = 128
SUBLANES = 8
DMA_QUEUES = 2
VMEM_LIMIT = 56 * 1024 * 1024

NT_DIMS = (((1,), (1,)), ((), ()))


def _params(*sem):
    return pltpu.CompilerParams(dimension_semantics=sem, vmem_limit_bytes=VMEM_LIMIT)


def _dot(a, b):
    return jnp.dot(a, b, preferred_element_type=F32)


def _split_bf16(a):
    hi = a.astype(BF16)
    lo = (a - hi.astype(F32)).astype(BF16)
    return hi, lo


def _gelu_tanh(x):
    return 0.5 * x * (1.0 + jnp.tanh(0.7978845608028654 * (x + 0.044715 * (x * x * x))))


def _sigmoid(x):
    return 1.0 / (1.0 + jnp.exp(-x))


def _rms(x):
    return x * lax.rsqrt(jnp.mean(x * x, axis=-1, keepdims=True) + EPS)


def _mod_kernel(c_ref, w_ref, b_ref, o_ref):
    c = c_ref[...]
    a = c * _sigmoid(c)
    ah, al = _split_bf16(a)
    wh, wl = _split_bf16(w_ref[0])
    o_ref[0] = _dot(ah, wh) + _dot(al, wh) + _dot(ah, wl) + b_ref[0]


def _mod_vectors(cvec, w_mod, b_mod):
    depth, d, n = w_mod.shape
    r = cvec.shape[0]
    tn = 1536
    return pl.pallas_call(
        _mod_kernel,
        grid=(depth, n // tn),
        in_specs=[pl.BlockSpec((r, d), lambda l, j: (0, 0)),
                  pl.BlockSpec((1, d, tn), lambda l, j: (l, 0, j)),
                  pl.BlockSpec((1, 1, tn), lambda l, j: (l, 0, j))],
        out_specs=pl.BlockSpec((1, r, tn), lambda l, j: (l, 0, j)),
        out_shape=jax.ShapeDtypeStruct((depth, r, n), F32),
        compiler_params=_params("parallel", "parallel"),
        name="mod_vectors",
    )(cvec, w_mod, b_mod.reshape(depth, 1, n))


def _group_rms(x, lo):
    sq = x * x
    s_lo = jnp.sum(jnp.where(lo, sq, 0.0), axis=-1, keepdims=True)
    s_hi = jnp.sum(jnp.where(lo, 0.0, sq), axis=-1, keepdims=True)
    inv = jnp.where(lo, lax.rsqrt(s_lo * (1.0 / B_QK_DIM) + EPS), lax.rsqrt(s_hi * (1.0 / B_QK_DIM) + EPS))
    return x * inv


def _in_proj_kernel(x_ref, mod_ref, nw_ref, w_ref, qw_ref, kw_ref, cos_ref, sin_ref, z_ref, k32_ref, v32_ref, h_scr,
                    *, n_col_tiles):
    j = pl.program_id(1)

    @pl.when(j == 0)
    def _():
        y = _rms(x_ref[...]) * nw_ref[...]
        h_scr[...] = (y * (1.0 + mod_ref[0, 1:2, :]) + mod_ref[0, 0:1, :]).astype(BF16)

    lane = lax.broadcasted_iota(jnp.int32, (1, LANES), 1)
    lo = lane < B_QK_DIM
    first_half = (lane % (2 * ROPE_FREQS)) < ROPE_FREQS
    q_scale = LOG2_E / math.sqrt(B_QK_DIM)
    q0, k0, v0 = (cb * COL_BLK // LANES for cb in (CB_Q, CB_K, CB_VAL))
    blocks_per_tile = z_ref.shape[1] // LANES
    pair = 2 * LANES

    def prep(x, w):
        y = _group_rms(x, lo) * w
        swapped = jnp.where(first_half, pltpu.roll(y, LANES - ROPE_FREQS, axis=1), pltpu.roll(y, ROPE_FREQS, axis=1))
        return y * cos_ref[...] + swapped * sin_ref[...]

    for jt in range(n_col_tiles):
        @pl.when(j == jt)
        def _(jt=jt):
            for sb in range(z_ref.shape[1] // pair):
                acc = _dot(h_scr[...], w_ref[:, sb * pair:(sb + 1) * pair])
                for half in range(2):
                    local = 2 * sb + half
                    g = jt * blocks_per_tile + local
                    blk = slice(local * LANES, (local + 1) * LANES)
                    a = acc[:, half * LANES:(half + 1) * LANES]
                    if q0 <= g < q0 + B_HEADS:
                        a = prep(a, qw_ref[...] * q_scale)
                    elif k0 <= g < k0 + B_HEADS:
                        a = prep(a, kw_ref[...])
                        k32_ref[:, (g - k0) * LANES:(g - k0 + 1) * LANES] = a
                    elif v0 <= g < v0 + B_HEADS:
                        v32_ref[:, (g - v0) * LANES:(g - v0 + 1) * LANES] = a
                    z_ref[:, blk] = a.astype(z_ref.dtype)


def _in_proj(x, mod, mod_row, norm_w, w_in, qw, kw, rope_tabs, n_ctx_tiles, pos_tiles, tm, tn):
    t, d = x.shape
    n = w_in.shape[1]
    assert tn % (2 * LANES) == 0
    for cb in (CB_K, CB_VAL):
        assert (cb * COL_BLK) // tn == (cb * COL_BLK + B_WIDTH - 1) // tn
    rope_row = lambda i, j: (jnp.where(i < n_ctx_tiles, 0, 1 + (i - n_ctx_tiles) % pos_tiles), 0)
    ctx_rows = lambda i, j: (jnp.minimum(i, n_ctx_tiles), 0)
    kv_shape = jax.ShapeDtypeStruct(((n_ctx_tiles + 1) * tm, B_WIDTH), F32)
    return pl.pallas_call(
        functools.partial(_in_proj_kernel, n_col_tiles=n // tn),
        grid=(t // tm, n // tn),
        in_specs=[pl.BlockSpec((tm, d), lambda i, j: (i, 0)),
                  pl.BlockSpec((1, 6, d), lambda i, j: (mod_row(i), 0, 0)),
                  pl.BlockSpec((1, d), lambda i, j: (0, 0)),
                  pl.BlockSpec((d, tn), lambda i, j: (0, j)),
                  pl.BlockSpec((1, LANES), lambda i, j: (0, 0)),
                  pl.BlockSpec((1, LANES), lambda i, j: (0, 0)),
                  pl.BlockSpec((tm, LANES), rope_row),
                  pl.BlockSpec((tm, LANES), rope_row)],
        out_specs=[pl.BlockSpec((tm, tn), lambda i, j: (i, j)),
                   pl.BlockSpec((tm, B_WIDTH), ctx_rows),
                   pl.BlockSpec((tm, B_WIDTH), ctx_rows)],
        out_shape=[jax.ShapeDtypeStruct((t, n), BF16), kv_shape, kv_shape],
        scratch_shapes=[pltpu.VMEM((tm, d), BF16)],
        compiler_params=_params("arbitrary", "arbitrary"),
        name="in_proj",
    )(x, mod, norm_w, w_in, qw, kw, *rope_tabs)


def _mixer_a_kernel(u_ref, v_ref, nw_ref, ws_ref, bias_ref, o_ref):
    tm = u_ref.shape[0]
    v = _rms(_gelu_tanh(v_ref[...].astype(F32))) * nw_ref[...]
    vb = v.astype(BF16)
    lo = lax.broadcasted_iota(jnp.int32, (1, LANES), 1) < A_GROUP_DIM
    for c in range(tm // CHUNK):
        rows = slice(c * CHUNK, (c + 1) * CHUNK)
        for j in range(A_WIDTH // LANES):
            cols = slice(j * LANES, (j + 1) * LANES)
            blk = vb[rows, cols]
            sv = jnp.where(lo, _dot(ws_ref[2 * j], blk), _dot(ws_ref[2 * j + 1], blk)) + bias_ref[:, cols]
            u = _gelu_tanh(u_ref[rows, cols].astype(F32))
            o_ref[rows, cols] = (u * sv).astype(o_ref.dtype)


def _mixer_a(z, norm_w, w_s, bias, tm):
    t = z.shape[0]
    return pl.pallas_call(
        _mixer_a_kernel,
        grid=(t // tm,),
        in_specs=[pl.BlockSpec((tm, A_WIDTH), lambda i: (i, CB_U)),
                  pl.BlockSpec((tm, A_WIDTH), lambda i: (i, CB_V)),
                  pl.BlockSpec((1, A_WIDTH), lambda i: (0, 0)),
                  pl.BlockSpec((A_GROUPS, CHUNK, CHUNK), lambda i: (0, 0, 0)),
                  pl.BlockSpec((CHUNK, A_WIDTH), lambda i: (0, 0))],
        out_specs=pl.BlockSpec((tm, A_WIDTH), lambda i: (i, 0)),
        out_shape=jax.ShapeDtypeStruct((t, A_WIDTH), BF16),
        compiler_params=_params("parallel"),
        name="mixer_a",
    )(z, z, norm_w, w_s, bias)


def _rope_tables(n_tokens, identity_rows):
    t = jnp.arange(n_tokens)
    row = (t // GRID_W).astype(F32)
    col = (t % GRID_W).astype(F32)
    freq = ROPE_THETA ** (-jnp.arange(ROPE_FREQS, dtype=F32) / ROPE_FREQS)
    ang_r = row[:, None] * freq[None, :]
    ang_c = col[:, None] * freq[None, :]
    cos = jnp.concatenate([jnp.cos(ang_r)] * 2 + [jnp.cos(ang_c)] * 2, axis=1)
    sin = jnp.concatenate([-jnp.sin(ang_r), jnp.sin(ang_r), -jnp.sin(ang_c), jnp.sin(ang_c)], axis=1)
    cos = jnp.concatenate([jnp.ones((identity_rows, LANES), F32), jnp.tile(cos, (1, 2))], axis=0)
    sin = jnp.concatenate([jnp.zeros((identity_rows, LANES), F32), jnp.tile(sin, (1, 2))], axis=0)
    return cos, sin


def _attn_kernel(*refs, lam_init, has_ctx, heads):
    lq_ref, sw_ref, q_ref, k_ref, v_ref = refs[:5]
    if has_ctx:
        ck_ref, cv_ref, o_ref, vt_scr, cvt_scr = refs[5:]
    else:
        o_ref, vt_scr = refs[5:]

    @pl.when(pl.program_id(2) == 0)
    def _():
        vt_scr[...] = v_ref[...].astype(F32).T.astype(BF16)
        if has_ctx:
            cvt_scr[...] = cv_ref[...].T.astype(BF16)

    lq = lq_ref[...]
    s01 = jnp.sum(lq[0:1] * lq[1:2], axis=-1, keepdims=True)
    s23 = jnp.sum(lq[2:3] * lq[3:4], axis=-1, keepdims=True)
    lam = jnp.exp(s01) - jnp.exp(s23) + lam_init
    lo = lax.broadcasted_iota(jnp.int32, (1, LANES), 1) < B_QK_DIM

    def scores(h, m):
        cols = slice(h * LANES, (h + 1) * LANES)
        q = q_ref[:, cols]
        zero = jnp.zeros_like(q)
        qm = jnp.where(lo, q, zero) if m == 0 else jnp.where(lo, zero, q)
        s_l = lax.dot_general(k_ref[:, cols], qm, NT_DIMS, preferred_element_type=F32)
        s_c = None
        if has_ctx:
            s_c = lax.dot_general(ck_ref[:, cols].astype(BF16), qm, NT_DIMS, preferred_element_type=F32)
        return s_l, s_c

    def weighted_values(h, m, s_l, s_c):
        cols = slice(h * LANES, (h + 1) * LANES)
        mx = jnp.max(s_l, axis=0, keepdims=True)
        if has_ctx:
            mx = jnp.maximum(mx, jnp.max(s_c, axis=0, keepdims=True))
        e_l = jnp.exp2(s_l - mx)
        den = jnp.sum(e_l, axis=0, keepdims=True)
        o_m = _dot(vt_scr[cols, :], e_l.astype(BF16))
        if has_ctx:
            e_c = jnp.exp2(s_c - mx)
            den = den + jnp.sum(e_c, axis=0, keepdims=True)
            o_m = o_m + _dot(cvt_scr[cols, :], e_c.astype(BF16))
        return o_m * ((1.0 if m == 0 else lam) / den)

    chains = [(h, m) for h in range(heads) for m in range(2)]
    pending = scores(*chains[0])
    outs = []
    for n, (h, m) in enumerate(chains):
        following = scores(*chains[n + 1]) if n + 1 < len(chains) else None
        outs.append(weighted_values(h, m, *pending))
        pending = following
        if m == 1:
            cols = slice(h * LANES, (h + 1) * LANES)
            o_t = outs[0] - outs[1]
            outs = []
            o_t = o_t * lax.rsqrt(jnp.mean(o_t * o_t, axis=0, keepdims=True) + EPS)
            o_ref[:, cols] = (o_t.T * (sw_ref[...] * (1.0 - lam_init))).astype(o_ref.dtype)


def _attention(z, row0, batch, seq, lam_qk, subln_w, lam_init, ctx, tq, heads):
    nq = seq // tq
    width = heads * LANES
    qcol0, kcol0, vcol0 = (cb * COL_BLK // width for cb in (CB_Q, CB_K, CB_VAL))
    qrow0 = row0 // tq
    krow0 = row0 // seq
    in_specs = [pl.BlockSpec((4, B_QK_DIM), lambda b, h, i: (0, 0)),
                pl.BlockSpec((1, LANES), lambda b, h, i: (0, 0)),
                pl.BlockSpec((tq, width), lambda b, h, i: (qrow0 + b * nq + i, qcol0 + h)),
                pl.BlockSpec((seq, width), lambda b, h, i: (krow0 + b, kcol0 + h)),
                pl.BlockSpec((seq, width), lambda b, h, i: (krow0 + b, vcol0 + h))]
    args = [lam_qk, subln_w, z, z, z]
    scratch = [pltpu.VMEM((width, seq), BF16)]
    if ctx is not None:
        cache_k, cache_v, layer = ctx
        past = cache_k.shape[2]
        spec = pl.BlockSpec((None, None, past, width), lambda b, h, i: (b, layer, 0, h))
        in_specs += [spec, spec]
        args += [cache_k, cache_v]
        scratch.append(pltpu.VMEM((width, past), BF16))
    return pl.pallas_call(
        functools.partial(_attn_kernel, lam_init=lam_init, has_ctx=ctx is not None, heads=heads),
        grid=(batch, B_HEADS // heads, nq),
        in_specs=in_specs,
        out_specs=pl.BlockSpec((tq, width), lambda b, h, i: (b * nq + i, h)),
        out_shape=jax.ShapeDtypeStruct((batch * seq, B_WIDTH), BF16),
        scratch_shapes=scratch,
        compiler_params=_params("parallel", "parallel", "arbitrary"),
        name="diff_attention",
    )(*args)


def _dft_tables(seq):
    def cos_sin(n):
        idx = jnp.arange(n, dtype=jnp.int32)
        ang = ((idx[:, None] * idx[None, :]) % n).astype(F32) * (2.0 * math.pi / n)
        s = 1.0 / math.sqrt(n)
        return jnp.cos(ang) * s, jnp.sin(ang) * s

    c_l, s_l = cos_sin(seq)
    c_c, s_c = cos_sin(C_GROUP_DIM)
    eye = jnp.eye(C_GROUPS, dtype=F32)
    chan = jnp.concatenate([jnp.kron(eye, c_c), jnp.kron(eye, s_c)], axis=1)
    return jnp.concatenate([c_l, -s_l], axis=1).astype(BF16), chan.astype(BF16)


def _fourier_kernel(f_ref, chan_ref, w_ref, o_ref, xcs_scr):
    seq = f_ref.shape[0]

    @pl.when(pl.program_id(1) == 0)
    def _():
        t = _dot(f_ref[...], chan_ref[...])
        xcs_scr[0:seq, :] = t[:, :C_WIDTH].astype(BF16)
        xcs_scr[seq:2 * seq, :] = t[:, C_WIDTH:].astype(BF16)

    o_ref[...] = _dot(w_ref[...], xcs_scr[...]).astype(o_ref.dtype)


def _fourier(z, row0, batch, seq, tables, tr):
    w, chan = tables
    nr = seq // tr
    rb0 = row0 // seq
    return pl.pallas_call(
        _fourier_kernel,
        grid=(batch, nr),
        in_specs=[pl.BlockSpec((seq, C_WIDTH), lambda b, r: (rb0 + b, CB_F)),
                  pl.BlockSpec((C_WIDTH, 2 * C_WIDTH), lambda b, r: (0, 0)),
                  pl.BlockSpec((tr, 2 * seq), lambda b, r: (r, 0))],
        out_specs=pl.BlockSpec((tr, C_WIDTH), lambda b, r: (b * nr + r, 0)),
        out_shape=jax.ShapeDtypeStruct((batch * seq, C_WIDTH), BF16),
        scratch_shapes=[pltpu.VMEM((2 * seq, C_WIDTH), BF16)],
        compiler_params=_params("parallel", "arbitrary"),
        name="fourier_mix",
    )(z, chan, w)


def _route(sb, s):
    rows_b = [sb[e:e + 1, :] for e in range(N_EXPERTS)]
    rows_s = [s[e:e + 1, :] for e in range(N_EXPERTS)]
    best, grp = None, None
    for g in range(N_EXPERT_GROUPS):
        r = rows_b[g * EXPERTS_PER_GROUP:(g + 1) * EXPERTS_PER_GROUP]
        pair = None
        for a in range(EXPERTS_PER_GROUP):
            for b in range(a + 1, EXPERTS_PER_GROUP):
                pair = r[a] + r[b] if pair is None else jnp.maximum(pair, r[a] + r[b])
        if g == 0:
            best, grp = pair, jnp.zeros(pair.shape, jnp.int32)
        else:
            upd = pair > best
            grp = jnp.where(upd, g, grp)
            best = jnp.where(upd, pair, best)
    within_b, within_s = [], []
    for j in range(EXPERTS_PER_GROUP):
        wb, ws = rows_b[j], rows_s[j]
        for g in range(1, N_EXPERT_GROUPS):
            wb = jnp.where(grp == g, rows_b[g * EXPERTS_PER_GROUP + j], wb)
            ws = jnp.where(grp == g, rows_s[g * EXPERTS_PER_GROUP + j], ws)
        within_b.append(wb)
        within_s.append(ws)

    def argmax_first(skip):
        bv, bi, bs = None, None, None
        for j in range(EXPERTS_PER_GROUP):
            v = within_b[j] if skip is None else jnp.where(skip == j, -jnp.inf, within_b[j])
            if j == 0:
                bv, bi, bs = v, jnp.zeros(v.shape, jnp.int32), within_s[0]
            else:
                upd = v > bv
                bv = jnp.where(upd, v, bv)
                bi = jnp.where(upd, j, bi)
                bs = jnp.where(upd, within_s[j], bs)
        return bi, bs

    i1, s1 = argmax_first(None)
    i2, s2 = argmax_first(i1)
    tot = s1 + s2
    eidx = jnp.concatenate([grp * EXPERTS_PER_GROUP + i1, grp * EXPERTS_PER_GROUP + i2], axis=0)
    wts = jnp.concatenate([s1 / tot, s2 / tot], axis=0)
    return eidx, wts


def _assignment_ranks(eidx, cnt_scr):
    tm = eidx.shape[1]
    expert = lax.broadcasted_iota(jnp.int32, (N_EXPERTS, tm), 0)
    onehot = jnp.concatenate([expert == eidx[0:1, :], expert == eidx[1:2, :]], axis=0).astype(F32)
    earlier = (lax.broadcasted_iota(jnp.int32, (tm, tm), 0)
               < lax.broadcasted_iota(jnp.int32, (tm, tm), 1)).astype(BF16)
    before = _dot(onehot.astype(BF16), earlier)
    total = jnp.sum(onehot, axis=1, keepdims=True)
    carried = cnt_scr[:, 0:1]
    first = onehot[:N_EXPERTS] * (carried + before[:N_EXPERTS])
    second = onehot[N_EXPERTS:] * (carried + total[:N_EXPERTS] + before[N_EXPERTS:])
    rank = jnp.concatenate([jnp.sum(first, axis=0, keepdims=True), jnp.sum(second, axis=0, keepdims=True)], axis=0)
    cnt_scr[...] = jnp.broadcast_to(carried + total[:N_EXPERTS] + total[N_EXPERTS:], cnt_scr.shape)
    return rank.astype(jnp.int32)


def _merge_kernel(x_ref, mod_ref, ya_ref, ybp_ref, ybs_ref, ycp_ref, ycs_ref, g_refs, wa_ref, wb_ref, wc_ref,
                  wo_ref, nw_ref, rw_ref, rb_ref, x1_ref, h2_ref, eidx_ref, wts_ref, rank_ref, cnt_ref, cnt_scr,
                  *, first_latent_tile):
    half = COL_BLK
    tile = pl.program_id(0)

    @pl.when(tile == 0)
    def _():
        cnt_scr[...] = jnp.zeros_like(cnt_scr)

    is_ctx = tile < first_latent_tile
    pa = _dot(ya_ref[...], wa_ref[...])
    pb = _dot(jnp.where(is_ctx, ybp_ref[...], ybs_ref[...]), wb_ref[...])
    pc = _dot(jnp.where(is_ctx, ycp_ref[...], ycs_ref[...]), wc_ref[...])
    parts = []
    for c in range(2):
        cols = slice(c * half, (c + 1) * half)
        m = (_sigmoid(g_refs[0 + c][...].astype(F32)) * pa[:, cols]
             + _sigmoid(g_refs[2 + c][...].astype(F32)) * pb[:, cols]
             + _sigmoid(g_refs[4 + c][...].astype(F32)) * pc[:, cols])
        parts.append(m.astype(BF16))
    merged = jnp.concatenate(parts, axis=1)
    x1 = x_ref[...] + mod_ref[0, 2:3, :] * _dot(merged, wo_ref[...])
    x1_ref[...] = x1
    h2 = _rms(x1) * nw_ref[...] * (1.0 + mod_ref[0, 4:5, :]) + mod_ref[0, 3:4, :]
    for c in range(SUBLANES):
        h2_ref[:, c, :] = h2[:, c * LANES:(c + 1) * LANES]
    hh, hl = _split_bf16(h2)
    rh, rl = _split_bf16(rw_ref[...])
    logits = (lax.dot_general(rh, hh, NT_DIMS, preferred_element_type=F32)
              + lax.dot_general(rh, hl, NT_DIMS, preferred_element_type=F32)
              + lax.dot_general(rl, hh, NT_DIMS, preferred_element_type=F32))
    s = _sigmoid(logits)
    eidx, wts = _route(s + rb_ref[...], s)
    eidx_ref[...] = eidx
    wts_ref[...] = wts
    rank_ref[...] = _assignment_ranks(eidx, cnt_scr)
    cnt_ref[...] = cnt_scr[...]


def _merge(x, mod, mod_row, ya, yb_ctx, yb_lat, yc_ctx, yc_lat, z, w_br_a, w_br_b, w_br_c, w_out, norm2_w,
           router_wt, router_b, tm):
    t, d = x.shape
    n_ctx = yb_ctx.shape[0] // tm
    last_ctx = n_ctx - 1
    const = lambda shape: pl.BlockSpec(shape, lambda i: (0,) * len(shape))
    ctx_rows = lambda i: (jnp.minimum(i, last_ctx), 0)
    lat_rows = lambda i: (jnp.maximum(i - n_ctx, 0), 0)
    g_specs = [pl.BlockSpec((tm, COL_BLK), functools.partial(lambda i, cb: (i, cb), cb=CB_G + n)) for n in range(6)]

    def body(x_ref, mod_ref, ya_ref, ybp_ref, ybs_ref, ycp_ref, ycs_ref, g0, g1, g2, g3, g4, g5, *rest):
        _merge_kernel(x_ref, mod_ref, ya_ref, ybp_ref, ybs_ref, ycp_ref, ycs_ref, (g0, g1, g2, g3, g4, g5), *rest,
                      first_latent_tile=n_ctx)

    token_major = pl.BlockSpec((TOP_K, tm), lambda i: (0, i))
    return pl.pallas_call(
        body,
        grid=(t // tm,),
        in_specs=[pl.BlockSpec((tm, d), lambda i: (i, 0)),
                  pl.BlockSpec((1, 6, d), lambda i: (mod_row(i), 0, 0)),
                  pl.BlockSpec((tm, A_WIDTH), lambda i: (i, 0)),
                  pl.BlockSpec((tm, B_WIDTH), ctx_rows),
                  pl.BlockSpec((tm, B_WIDTH), lat_rows),
                  pl.BlockSpec((tm, C_WIDTH), ctx_rows),
                  pl.BlockSpec((tm, C_WIDTH), lat_rows),
                  *g_specs,
                  const(w_br_a.shape), const(w_br_b.shape), const(w_br_c.shape), const(w_out.shape),
                  const((1, d)), const(router_wt.shape), const((N_EXPERTS, 1))],
        out_specs=[pl.BlockSpec((tm, d), lambda i: (i, 0)),
                   pl.BlockSpec((tm, SUBLANES, LANES), lambda i: (i, 0, 0)),
                   token_major, token_major, token_major,
                   const((N_EXPERTS, LANES))],
        out_shape=[jax.ShapeDtypeStruct((t, d), F32),
                   jax.ShapeDtypeStruct((t, SUBLANES, LANES), F32),
                   jax.ShapeDtypeStruct((TOP_K, t), jnp.int32),
                   jax.ShapeDtypeStruct((TOP_K, t), F32),
                   jax.ShapeDtypeStruct((TOP_K, t), jnp.int32),
                   jax.ShapeDtypeStruct((N_EXPERTS, LANES), F32)],
        scratch_shapes=[pltpu.VMEM((N_EXPERTS, LANES), F32)],
        compiler_params=_params("arbitrary"),
        name="merge_router",
    )(x, mod, ya, yb_ctx, yb_lat, yc_ctx, yc_lat, z, z, z, z, z, z, w_br_a, w_br_b, w_br_c, w_out, norm2_w,
      router_wt, router_b)


def _expert_kernel(dest_ref, te_ref, nv_ref, nu_ref, h_hbm, wg_ref, wu_ref, wd_ref, y_ref, src_ref, hbuf0, hbuf1, sem,
                   *, tm, n_tok):
    i = pl.program_id(0)
    n_used = nu_ref[0]
    bufs = (hbuf0, hbuf1)

    def row_copy(tile, r, slot):
        base = tile * tm
        last_valid = nv_ref[tile] - 1
        tok = src_ref[base + jnp.minimum(r, last_valid)]
        return pltpu.make_async_copy(h_hbm.at[tok], bufs[slot].at[r], sem.at[slot])

    def wait_rows(slot):
        pltpu.make_async_copy(bufs[slot], bufs[slot], sem.at[slot]).wait()

    def dense_rows(slot):
        return jnp.concatenate([bufs[slot][:, c, :] for c in range(SUBLANES)], axis=1)

    @pl.when(i == 0)
    def _():
        def invert(t, carry):
            src_ref[dest_ref[t]] = t
            src_ref[dest_ref[n_tok + t]] = t
            return carry

        lax.fori_loop(0, n_tok, invert, 0, unroll=8)
        for r in range(tm):
            row_copy(0, r, 0).start(priority=r % DMA_QUEUES)

    for slot in range(2):
        @pl.when(jnp.logical_and(i < n_used, i % 2 == slot))
        def _():
            wait_rows(slot)
            h = dense_rows(slot).astype(BF16)
            nxt = jnp.minimum(i + 1, n_used - 1)
            for r in range(tm):
                row_copy(nxt, r, 1 - slot).start(priority=r % DMA_QUEUES)
            g = _dot(h, wg_ref[...].astype(BF16))
            a = (g * _sigmoid(g)) * _dot(h, wu_ref[...].astype(BF16))
            y_ref[...] = _dot(a.astype(BF16), wd_ref[...].astype(BF16)).astype(y_ref.dtype)

            @pl.when(i == n_used - 1)
            def _():
                wait_rows(1 - slot)

    @pl.when(i >= n_used)
    def _():
        y_ref[...] = jnp.zeros_like(y_ref)


def _experts(h2, dest, tile_expert, valid_rows, n_used, w_g, w_u, w_d, layer, tm):
    n_tok = h2.shape[0]
    d, de = w_g.shape[-2:]
    assert d == SUBLANES * LANES
    n_tiles = tile_expert.shape[0]
    weight = lambda shape: pl.BlockSpec((None, None) + shape, lambda i, dst, te, nv, nu: (layer, te[i], 0, 0))
    grid_spec = pltpu.PrefetchScalarGridSpec(
        num_scalar_prefetch=4,
        grid=(n_tiles,),
        in_specs=[pl.BlockSpec(memory_space=pl.ANY), weight((d, de)), weight((d, de)), weight((de, d))],
        out_specs=pl.BlockSpec((tm, d), lambda i, dst, te, nv, nu: (i, 0)),
        scratch_shapes=[pltpu.SMEM((n_tiles * tm,), jnp.int32),
                        pltpu.VMEM((tm, SUBLANES, LANES), F32), pltpu.VMEM((tm, SUBLANES, LANES), F32),
                        pltpu.SemaphoreType.DMA((2,))],
    )
    return pl.pallas_call(
        functools.partial(_expert_kernel, tm=tm, n_tok=n_tok),
        grid_spec=grid_spec,
        out_shape=jax.ShapeDtypeStruct((n_tiles * tm, d), BF16),
        compiler_params=_params("arbitrary"),
        name="experts",
    )(dest.reshape(TOP_K * n_tok), tile_expert, valid_rows, n_used, h2, w_g, w_u, w_d)


def _dispatch_plan(eidx, rank, counts, tm):
    t = eidx.shape[1]
    n_tiles = (TOP_K * t + N_EXPERTS * (tm - 1)) // tm
    tiles_per = (counts + tm - 1) // tm
    tile_end = jnp.cumsum(tiles_per)
    row_start = (tile_end - tiles_per) * tm
    experts = jnp.arange(N_EXPERTS, dtype=jnp.int32)
    start_of = jnp.sum(jnp.where(eidx[None] == experts[:, None, None], row_start[:, None, None], 0), axis=0)
    dest = rank + start_of
    tile_ids = jnp.arange(n_tiles, dtype=jnp.int32)
    tile_expert = jnp.minimum(jnp.sum((tile_end[None, :] <= tile_ids[:, None]).astype(jnp.int32), axis=1),
                              N_EXPERTS - 1)
    row_end = jnp.take(row_start + counts, tile_expert)
    valid_rows = jnp.clip(row_end - tile_ids * tm, 0, tm).astype(jnp.int32)
    n_used = tile_end[-1:].astype(jnp.int32)
    return dest, tile_expert.astype(jnp.int32), valid_rows, n_used


def _combine_kernel(x_ref, mod_ref, y0_ref, y1_ref, w_ref, o_ref):
    w = w_ref[...]
    y = w[:, 0:1] * y0_ref[...].astype(F32) + w[:, 1:2] * y1_ref[...].astype(F32)
    o_ref[...] = x_ref[...] + mod_ref[0, 5:6, :] * y


def _combine(x, mod, mod_row, y0, y1, w_cols, tm, row0, rows):
    d = x.shape[1]
    rb0 = row0 // tm
    token_rows = lambda width: pl.BlockSpec((tm, width), lambda i: (rb0 + i, 0))
    return pl.pallas_call(
        _combine_kernel,
        grid=(rows // tm,),
        in_specs=[token_rows(d),
                  pl.BlockSpec((1, 6, d), lambda i: (mod_row(rb0 + i), 0, 0)),
                  token_rows(d), token_rows(d), token_rows(TOP_K)],
        out_specs=pl.BlockSpec((tm, d), lambda i: (i, 0)),
        out_shape=jax.ShapeDtypeStruct((rows, d), F32),
        compiler_params=_params("parallel"),
        name="combine",
    )(x, mod, y0, y1, w_cols)


def _pick_tile(preferred, *sizes):
    tile = preferred
    while any(s % tile for s in sizes):
        tile //= 2
    return tile


def kernel(x_prompt, x_sample, cache_k, cache_v, c, c_ctx, w_mod, b_mod, norm1_w, norm2_w, w_in, a_norm_w, a_w_s, a_b_s, q_norm_w, k_norm_w, lambda_qk, subln_w, w_br_a, w_br_b, w_br_c, w_out, router_w, router_b, w_e_gate, w_e_up, w_e_down):
    batch, seq, d = x_prompt.shape
    dec_batch, dec_seq, _ = x_sample.shape
    depth = w_mod.shape[0]
    past = cache_k.shape[2]
    tp, ts = batch * seq, dec_batch * dec_seq
    t = tp + ts
    assert tp % dec_seq == 0 and dec_seq % seq == 0 and seq % CHUNK == 0 and dec_seq % GRID_W == 0

    tm = _pick_tile(1024, tp, dec_seq)
    tm_small = _pick_tile(512, tp, dec_seq)
    tm_exp = 512
    tn_in = 1536

    def mod_row_for(tile):
        first_lat = tp // tile
        per_batch = dec_seq // tile
        return lambda i: jnp.where(i < first_lat, 0, 1 + (i - first_lat) // per_batch)

    n_rows = 1 + dec_batch
    r_pad = -(-n_rows // 8) * 8
    cvec = jnp.zeros((r_pad, d), F32).at[0].set(c_ctx).at[1:n_rows].set(c)
    mod_all = _mod_vectors(cvec, w_mod, b_mod)[:, :n_rows].reshape(depth, n_rows, 6, d)

    rope_tabs = _rope_tables(dec_seq, tm)
    dft_p = _dft_tables(seq)
    dft_s = _dft_tables(dec_seq)
    cache_k2 = cache_k.reshape(dec_batch, depth, past, B_HEADS * 2 * B_QK_DIM)
    cache_v2 = cache_v.reshape(dec_batch, depth, past, B_WIDTH)
    router_wt = router_w.T
    router_b2 = router_b.reshape(N_EXPERTS, 1)

    x = jnp.concatenate([x_prompt.reshape(tp, d), x_sample.reshape(ts, d)], axis=0)
    new_k, new_v = [], []
    for l in range(depth):
        mod = mod_all[l]
        lam_init = 0.8 - 0.6 * math.exp(-0.3 * l)
        qw = jnp.tile(q_norm_w[l], 2).reshape(1, LANES)
        kw = jnp.tile(k_norm_w[l], 2).reshape(1, LANES)
        sw = subln_w[l].reshape(1, LANES)
        z, k32, v32 = _in_proj(x, mod, mod_row_for(tm), norm1_w[l].reshape(1, d), w_in[l].astype(BF16), qw, kw,
                               rope_tabs, tp // tm, dec_seq // tm, tm, tn_in)

        bias_a = jnp.repeat(a_b_s[l].T, A_GROUP_DIM, axis=1)
        ya = _mixer_a(z, a_norm_w[l].reshape(1, A_WIDTH), a_w_s[l].astype(BF16), bias_a, tm_small)

        yb_p = _attention(z, 0, batch, seq, lambda_qk[l], sw, lam_init, None, seq, B_HEADS)
        yb_s = _attention(z, tp, dec_batch, dec_seq, lambda_qk[l], sw, lam_init,
                          (cache_k2, cache_v2, l), _pick_tile(512, dec_seq), 1)

        yc_p = _fourier(z, 0, batch, seq, dft_p, seq)
        yc_s = _fourier(z, tp, dec_batch, dec_seq, dft_s, _pick_tile(512, dec_seq))

        x1, h2, eidx, wts, rank, cnt = _merge(
            x, mod, mod_row_for(tm_small), ya, yb_p, yb_s, yc_p, yc_s, z,
            w_br_a[l].astype(BF16), w_br_b[l].astype(BF16), w_br_c[l].astype(BF16), w_out[l].astype(BF16),
            norm2_w[l].reshape(1, d), router_wt, router_b2, tm_small)

        dest, tile_expert, valid_rows, n_used = _dispatch_plan(eidx, rank, cnt[:, 0].astype(jnp.int32), tm_exp)
        yg = _experts(h2, dest, tile_expert, valid_rows, n_used, w_e_gate, w_e_up, w_e_down, l, tm_exp)
        y0 = yg.at[dest[0]].get(mode="promise_in_bounds")
        y1 = yg.at[dest[1]].get(mode="promise_in_bounds")
        combine = functools.partial(_combine, x1, mod, mod_row_for(tm_small), y0, y1, wts.T, tm_small)
        if l + 1 < depth:
            x = combine(0, t)
        else:
            y_prompt = combine(0, tp).reshape(batch, seq, d)
            y_sample = combine(tp, ts).reshape(dec_batch, dec_seq, d)

        new_k.append(k32[:tp].reshape(batch, seq, B_HEADS, 2 * B_QK_DIM))
        new_v.append(v32[:tp].reshape(batch, seq, B_HEADS, B_V_DIM))

    return (y_prompt, y_sample, jnp.stack(new_k, axis=1), jnp.stack(new_v, axis=1))
```

```python
import functools
import math

import jax
import jax.numpy as jnp
from jax import lax
from jax.experimental import pallas as pl
from jax.experimental.pallas import tpu as pltpu

F32 = jnp.float32
BF16 = jnp.bfloat16

EPS = 1e-6
GRID_W = 64
CHUNK = 128
A_GROUPS = 8
A_GROUP_DIM = 64
A_WIDTH = A_GROUPS * A_GROUP_DIM
B_HEADS = 8
B_QK_DIM = 64
B_V_DIM = 2 * B_QK_DIM
B_WIDTH = B_HEADS * B_V_DIM
ROPE_THETA = 10000.0
ROPE_FREQS = B_QK_DIM // 4
C_GROUPS = 4
C_GROUP_DIM = 128
C_WIDTH = C_GROUPS * C_GROUP_DIM
N_EXPERTS = 16
N_EXPERT_GROUPS = 4
EXPERTS_PER_GROUP = N_EXPERTS // N_EXPERT_GROUPS
TOP_K = 2

COL_BLK = 512
CB_U, CB_V, CB_Q, CB_K, CB_VAL, CB_F, CB_G = 0, 1, 2, 4, 6, 8, 9

LOG2_E = 1.4426950408889634
LANES = 128
VMEM_LIMIT = 56 * 1024 * 1024

NT_DIMS = (((1,), (1,)), ((), ()))


def _params(*sem):
    return pltpu.CompilerParams(dimension_semantics=sem, vmem_limit_bytes=VMEM_LIMIT)


def _dot(a, b):
    return jnp.dot(a, b, preferred_element_type=F32)


def _split_bf16(a):
    hi = a.astype(BF16)
    lo = (a - hi.astype(F32)).astype(BF16)
    return hi, lo


def _gelu_tanh(x):
    return 0.5 * x * (1.0 + jnp.tanh(0.7978845608028654 * (x + 0.044715 * (x * x * x))))


def _sigmoid(x):
    return 1.0 / (1.0 + jnp.exp(-x))


def _rms(x):
    return x * lax.rsqrt(jnp.mean(x * x, axis=-1, keepdims=True) + EPS)


def _mod_kernel(c_ref, w_ref, b_ref, o_ref):
    c = c_ref[...]
    a = c * _sigmoid(c)
    ah, al = _split_bf16(a)
    wh, wl = _split_bf16(w_ref[0])
    o_ref[0] = _dot(ah, wh) + _dot(al, wh) + _dot(ah, wl) + b_ref[0]


def _mod_vectors(cvec, w_mod, b_mod):
    depth, d, n = w_mod.shape
    r = cvec.shape[0]
    tn = 1536
    return pl.pallas_call(
        _mod_kernel,
        grid=(depth, n // tn),
        in_specs=[pl.BlockSpec((r, d), lambda l, j: (0, 0)),
                  pl.BlockSpec((1, d, tn), lambda l, j: (l, 0, j)),
                  pl.BlockSpec((1, 1, tn), lambda l, j: (l, 0, j))],
        out_specs=pl.BlockSpec((1, r, tn), lambda l, j: (l, 0, j)),
        out_shape=jax.ShapeDtypeStruct((depth, r, n), F32),
        compiler_params=_params("parallel", "parallel"),
        name="mod_vectors",
    )(cvec, w_mod, b_mod.reshape(depth, 1, n))


def _group_rms(x, lo):
    sq = x * x
    s_lo = jnp.sum(jnp.where(lo, sq, 0.0), axis=-1, keepdims=True)
    s_hi = jnp.sum(jnp.where(lo, 0.0, sq), axis=-1, keepdims=True)
    inv = jnp.where(lo, lax.rsqrt(s_lo * (1.0 / B_QK_DIM) + EPS), lax.rsqrt(s_hi * (1.0 / B_QK_DIM) + EPS))
    return x * inv


def _in_proj_kernel(x_ref, mod_ref, nw_ref, w_ref, qw_ref, kw_ref, cos_ref, sin_ref, z_ref, k32_ref, v32_ref, h_scr,
                    *, n_col_tiles):
    j = pl.program_id(1)

    @pl.when(j == 0)
    def _():
        y = _rms(x_ref[...]) * nw_ref[...]
        h_scr[...] = (y * (1.0 + mod_ref[0, 1:2, :]) + mod_ref[0, 0:1, :]).astype(BF16)

    lane = lax.broadcasted_iota(jnp.int32, (1, LANES), 1)
    lo = lane < B_QK_DIM
    first_half = (lane % (2 * ROPE_FREQS)) < ROPE_FREQS
    q_scale = LOG2_E / math.sqrt(B_QK_DIM)
    q0, k0, v0 = (cb * COL_BLK // LANES for cb in (CB_Q, CB_K, CB_VAL))
    blocks_per_tile = z_ref.shape[1] // LANES
    pair = 2 * LANES

    def prep(x, w):
        y = _group_rms(x, lo) * w
        swapped = jnp.where(first_half, pltpu.roll(y, LANES - ROPE_FREQS, axis=1), pltpu.roll(y, ROPE_FREQS, axis=1))
        return y * cos_ref[...] + swapped * sin_ref[...]

    for jt in range(n_col_tiles):
        @pl.when(j == jt)
        def _(jt=jt):
            for sb in range(z_ref.shape[1] // pair):
                acc = _dot(h_scr[...], w_ref[:, sb * pair:(sb + 1) * pair])
                for half in range(2):
                    local = 2 * sb + half
                    g = jt * blocks_per_tile + local
                    blk = slice(local * LANES, (local + 1) * LANES)
                    a = acc[:, half * LANES:(half + 1) * LANES]
                    if q0 <= g < q0 + B_HEADS:
                        a = prep(a, qw_ref[...] * q_scale)
                    elif k0 <= g < k0 + B_HEADS:
                        a = prep(a, kw_ref[...])
                        k32_ref[:, (g - k0) * LANES:(g - k0 + 1) * LANES] = a
                    elif v0 <= g < v0 + B_HEADS:
                        v32_ref[:, (g - v0) * LANES:(g - v0 + 1) * LANES] = a
                    z_ref[:, blk] = a.astype(z_ref.dtype)


def _in_proj(x, mod, mod_row, norm_w, w_in, qw, kw, rope_tabs, n_ctx_tiles, pos_tiles, tm, tn):
    t, d = x.shape
    n = w_in.shape[1]
    assert tn % (2 * LANES) == 0
    for cb in (CB_K, CB_VAL):
        assert (cb * COL_BLK) // tn == (cb * COL_BLK + B_WIDTH - 1) // tn
    rope_row = lambda i, j: (jnp.where(i < n_ctx_tiles, 0, 1 + (i - n_ctx_tiles) % pos_tiles), 0)
    ctx_rows = lambda i, j: (jnp.minimum(i, n_ctx_tiles), 0)
    kv_shape = jax.ShapeDtypeStruct(((n_ctx_tiles + 1) * tm, B_WIDTH), F32)
    return pl.pallas_call(
        functools.partial(_in_proj_kernel, n_col_tiles=n // tn),
        grid=(t // tm, n // tn),
        in_specs=[pl.BlockSpec((tm, d), lambda i, j: (i, 0)),
                  pl.BlockSpec((1, 6, d), lambda i, j: (mod_row(i), 0, 0)),
                  pl.BlockSpec((1, d), lambda i, j: (0, 0)),
                  pl.BlockSpec((d, tn), lambda i, j: (0, j)),
                  pl.BlockSpec((1, LANES), lambda i, j: (0, 0)),
                  pl.BlockSpec((1, LANES), lambda i, j: (0, 0)),
                  pl.BlockSpec((tm, LANES), rope_row),
                  pl.BlockSpec((tm, LANES), rope_row)],
        out_specs=[pl.BlockSpec((tm, tn), lambda i, j: (i, j)),
                   pl.BlockSpec((tm, B_WIDTH), ctx_rows),
                   pl.BlockSpec((tm, B_WIDTH), ctx_rows)],
        out_shape=[jax.ShapeDtypeStruct((t, n), BF16), kv_shape, kv_shape],
        scratch_shapes=[pltpu.VMEM((tm, d), BF16)],
        compiler_params=_params("arbitrary", "arbitrary"),
        name="in_proj",
    )(x, mod, norm_w, w_in, qw, kw, *rope_tabs)


def _mixer_a_kernel(u_ref, v_ref, nw_ref, ws_ref, bias_ref, o_ref):
    tm = u_ref.shape[0]
    v = _rms(_gelu_tanh(v_ref[...].astype(F32))) * nw_ref[...]
    vb = v.astype(BF16)
    lo = lax.broadcasted_iota(jnp.int32, (1, LANES), 1) < A_GROUP_DIM
    for c in range(tm // CHUNK):
        rows = slice(c * CHUNK, (c + 1) * CHUNK)
        for j in range(A_WIDTH // LANES):
            cols = slice(j * LANES, (j + 1) * LANES)
            blk = vb[rows, cols]
            sv = jnp.where(lo, _dot(ws_ref[2 * j], blk), _dot(ws_ref[2 * j + 1], blk)) + bias_ref[:, cols]
            u = _gelu_tanh(u_ref[rows, cols].astype(F32))
            o_ref[rows, cols] = (u * sv).astype(o_ref.dtype)


def _mixer_a(z, norm_w, w_s, bias, tm):
    t = z.shape[0]
    return pl.pallas_call(
        _mixer_a_kernel,
        grid=(t // tm,),
        in_specs=[pl.BlockSpec((tm, A_WIDTH), lambda i: (i, CB_U)),
                  pl.BlockSpec((tm, A_WIDTH), lambda i: (i, CB_V)),
                  pl.BlockSpec((1, A_WIDTH), lambda i: (0, 0)),
                  pl.BlockSpec((A_GROUPS, CHUNK, CHUNK), lambda i: (0, 0, 0)),
                  pl.BlockSpec((CHUNK, A_WIDTH), lambda i: (0, 0))],
        out_specs=pl.BlockSpec((tm, A_WIDTH), lambda i: (i, 0)),
        out_shape=jax.ShapeDtypeStruct((t, A_WIDTH), BF16),
        compiler_params=_params("parallel"),
        name="mixer_a",
    )(z, z, norm_w, w_s, bias)


def _rope_tables(n_tokens, identity_rows):
    t = jnp.arange(n_tokens)
    row = (t // GRID_W).astype(F32)
    col = (t % GRID_W).astype(F32)
    freq = ROPE_THETA ** (-jnp.arange(ROPE_FREQS, dtype=F32) / ROPE_FREQS)
    ang_r = row[:, None] * freq[None, :]
    ang_c = col[:, None] * freq[None, :]
    cos = jnp.concatenate([jnp.cos(ang_r)] * 2 + [jnp.cos(ang_c)] * 2, axis=1)
    sin = jnp.concatenate([-jnp.sin(ang_r), jnp.sin(ang_r), -jnp.sin(ang_c), jnp.sin(ang_c)], axis=1)
    cos = jnp.concatenate([jnp.ones((identity_rows, LANES), F32), jnp.tile(cos, (1, 2))], axis=0)
    sin = jnp.concatenate([jnp.zeros((identity_rows, LANES), F32), jnp.tile(sin, (1, 2))], axis=0)
    return cos, sin


def _attn_kernel(*refs, lam_init, has_ctx, heads):
    lq_ref, sw_ref, q_ref, k_ref, v_ref = refs[:5]
    if has_ctx:
        ck_ref, cv_ref, o_ref, vt_scr, cvt_scr = refs[5:]
    else:
        o_ref, vt_scr = refs[5:]

    @pl.when(pl.program_id(2) == 0)
    def _():
        vt_scr[...] = v_ref[...].astype(F32).T.astype(BF16)
        if has_ctx:
            cvt_scr[...] = cv_ref[...].T.astype(BF16)

    lq = lq_ref[...]
    s01 = jnp.sum(lq[0:1] * lq[1:2], axis=-1, keepdims=True)
    s23 = jnp.sum(lq[2:3] * lq[3:4], axis=-1, keepdims=True)
    lam = jnp.exp(s01) - jnp.exp(s23) + lam_init
    lo = lax.broadcasted_iota(jnp.int32, (1, LANES), 1) < B_QK_DIM

    def scores(h, m):
        cols = slice(h * LANES, (h + 1) * LANES)
        q = q_ref[:, cols]
        zero = jnp.zeros_like(q)
        qm = jnp.where(lo, q, zero) if m == 0 else jnp.where(lo, zero, q)
        s_l = lax.dot_general(k_ref[:, cols], qm, NT_DIMS, preferred_element_type=F32)
        s_c = None
        if has_ctx:
            s_c = lax.dot_general(ck_ref[:, cols].astype(BF16), qm, NT_DIMS, preferred_element_type=F32)
        return s_l, s_c

    def weighted_values(h, m, s_l, s_c):
        cols = slice(h * LANES, (h + 1) * LANES)
        mx = jnp.max(s_l, axis=0, keepdims=True)
        if has_ctx:
            mx = jnp.maximum(mx, jnp.max(s_c, axis=0, keepdims=True))
        e_l = jnp.exp2(s_l - mx)
        den = jnp.sum(e_l, axis=0, keepdims=True)
        o_m = _dot(vt_scr[cols, :], e_l.astype(BF16))
        if has_ctx:
            e_c = jnp.exp2(s_c - mx)
            den = den + jnp.sum(e_c, axis=0, keepdims=True)
            o_m = o_m + _dot(cvt_scr[cols, :], e_c.astype(BF16))
        return o_m * ((1.0 if m == 0 else lam) / den)

    chains = [(h, m) for h in range(heads) for m in range(2)]
    pending = scores(*chains[0])
    outs = []
    for n, (h, m) in enumerate(chains):
        following = scores(*chains[n + 1]) if n + 1 < len(chains) else None
        outs.append(weighted_values(h, m, *pending))
        pending = following
        if m == 1:
            cols = slice(h * LANES, (h + 1) * LANES)
            o_t = outs[0] - outs[1]
            outs = []
            o_t = o_t * lax.rsqrt(jnp.mean(o_t * o_t, axis=0, keepdims=True) + EPS)
            o_ref[:, cols] = (o_t.T * (sw_ref[...] * (1.0 - lam_init))).astype(o_ref.dtype)


def _attention(z, row0, batch, seq, lam_qk, subln_w, lam_init, ctx, tq, heads):
    nq = seq // tq
    width = heads * LANES
    qcol0, kcol0, vcol0 = (cb * COL_BLK // width for cb in (CB_Q, CB_K, CB_VAL))
    qrow0 = row0 // tq
    krow0 = row0 // seq
    in_specs = [pl.BlockSpec((4, B_QK_DIM), lambda b, h, i: (0, 0)),
                pl.BlockSpec((1, LANES), lambda b, h, i: (0, 0)),
                pl.BlockSpec((tq, width), lambda b, h, i: (qrow0 + b * nq + i, qcol0 + h)),
                pl.BlockSpec((seq, width), lambda b, h, i: (krow0 + b, kcol0 + h)),
                pl.BlockSpec((seq, width), lambda b, h, i: (krow0 + b, vcol0 + h))]
    args = [lam_qk, subln_w, z, z, z]
    scratch = [pltpu.VMEM((width, seq), BF16)]
    if ctx is not None:
        cache_k, cache_v, layer = ctx
        past = cache_k.shape[2]
        spec = pl.BlockSpec((None, None, past, width), lambda b, h, i: (b, layer, 0, h))
        in_specs += [spec, spec]
        args += [cache_k, cache_v]
        scratch.append(pltpu.VMEM((width, past), BF16))
    return pl.pallas_call(
        functools.partial(_attn_kernel, lam_init=lam_init, has_ctx=ctx is not None, heads=heads),
        grid=(batch, B_HEADS // heads, nq),
        in_specs=in_specs,
        out_specs=pl.BlockSpec((tq, width), lambda b, h, i: (b * nq + i, h)),
        out_shape=jax.ShapeDtypeStruct((batch * seq, B_WIDTH), BF16),
        scratch_shapes=scratch,
        compiler_params=_params("parallel", "parallel", "arbitrary"),
        name="diff_attention",
    )(*args)


def _dft_tables(seq):
    def cos_sin(n):
        idx = jnp.arange(n, dtype=jnp.int32)
        ang = ((idx[:, None] * idx[None, :]) % n).astype(F32) * (2.0 * math.pi / n)
        s = 1.0 / math.sqrt(n)
        return jnp.cos(ang) * s, jnp.sin(ang) * s

    c_l, s_l = cos_sin(seq)
    c_c, s_c = cos_sin(C_GROUP_DIM)
    eye = jnp.eye(C_GROUPS, dtype=F32)
    chan = jnp.concatenate([jnp.kron(eye, c_c), jnp.kron(eye, s_c)], axis=1)
    return jnp.concatenate([c_l, -s_l], axis=1).astype(BF16), chan.astype(BF16)


def _fourier_kernel(f_ref, chan_ref, w_ref, o_ref, xcs_scr):
    seq = f_ref.shape[0]

    @pl.when(pl.program_id(1) == 0)
    def _():
        t = _dot(f_ref[...], chan_ref[...])
        xcs_scr[0:seq, :] = t[:, :C_WIDTH].astype(BF16)
        xcs_scr[seq:2 * seq, :] = t[:, C_WIDTH:].astype(BF16)

    o_ref[...] = _dot(w_ref[...], xcs_scr[...]).astype(o_ref.dtype)


def _fourier(z, row0, batch, seq, tables, tr):
    w, chan = tables
    nr = seq // tr
    rb0 = row0 // seq
    return pl.pallas_call(
        _fourier_kernel,
        grid=(batch, nr),
        in_specs=[pl.BlockSpec((seq, C_WIDTH), lambda b, r: (rb0 + b, CB_F)),
                  pl.BlockSpec((C_WIDTH, 2 * C_WIDTH), lambda b, r: (0, 0)),
                  pl.BlockSpec((tr, 2 * seq), lambda b, r: (r, 0))],
        out_specs=pl.BlockSpec((tr, C_WIDTH), lambda b, r: (b * nr + r, 0)),
        out_shape=jax.ShapeDtypeStruct((batch * seq, C_WIDTH), BF16),
        scratch_shapes=[pltpu.VMEM((2 * seq, C_WIDTH), BF16)],
        compiler_params=_params("parallel", "arbitrary"),
        name="fourier_mix",
    )(z, chan, w)


def _route(sb, s):
    rows_b = [sb[e:e + 1, :] for e in range(N_EXPERTS)]
    rows_s = [s[e:e + 1, :] for e in range(N_EXPERTS)]
    best, grp = None, None
    for g in range(N_EXPERT_GROUPS):
        r = rows_b[g * EXPERTS_PER_GROUP:(g + 1) * EXPERTS_PER_GROUP]
        pair = None
        for a in range(EXPERTS_PER_GROUP):
            for b in range(a + 1, EXPERTS_PER_GROUP):
                pair = r[a] + r[b] if pair is None else jnp.maximum(pair, r[a] + r[b])
        if g == 0:
            best, grp = pair, jnp.zeros(pair.shape, jnp.int32)
        else:
            upd = pair > best
            grp = jnp.where(upd, g, grp)
            best = jnp.where(upd, pair, best)
    within_b, within_s = [], []
    for j in range(EXPERTS_PER_GROUP):
        wb, ws = rows_b[j], rows_s[j]
        for g in range(1, N_EXPERT_GROUPS):
            wb = jnp.where(grp == g, rows_b[g * EXPERTS_PER_GROUP + j], wb)
            ws = jnp.where(grp == g, rows_s[g * EXPERTS_PER_GROUP + j], ws)
        within_b.append(wb)
        within_s.append(ws)

    def argmax_first(skip):
        bv, bi, bs = None, None, None
        for j in range(EXPERTS_PER_GROUP):
            v = within_b[j] if skip is None else jnp.where(skip == j, -jnp.inf, within_b[j])
            if j == 0:
                bv, bi, bs = v, jnp.zeros(v.shape, jnp.int32), within_s[0]
            else:
                upd = v > bv
                bv = jnp.where(upd, v, bv)
                bi = jnp.where(upd, j, bi)
                bs = jnp.where(upd, within_s[j], bs)
        return bi, bs

    i1, s1 = argmax_first(None)
    i2, s2 = argmax_first(i1)
    tot = s1 + s2
    eidx = jnp.concatenate([grp * EXPERTS_PER_GROUP + i1, grp * EXPERTS_PER_GROUP + i2], axis=0)
    wts = jnp.concatenate([s1 / tot, s2 / tot], axis=0)
    return eidx, wts


def _assignment_ranks(eidx, cnt_scr):
    tm = eidx.shape[1]
    expert = lax.broadcasted_iota(jnp.int32, (N_EXPERTS, tm), 0)
    onehot = jnp.concatenate([expert == eidx[0:1, :], expert == eidx[1:2, :]], axis=0).astype(F32)
    earlier = (lax.broadcasted_iota(jnp.int32, (tm, tm), 0)
               < lax.broadcasted_iota(jnp.int32, (tm, tm), 1)).astype(BF16)
    before = _dot(onehot.astype(BF16), earlier)
    total = jnp.sum(onehot, axis=1, keepdims=True)
    carried = cnt_scr[:, 0:1]
    first = onehot[:N_EXPERTS] * (carried + before[:N_EXPERTS])
    second = onehot[N_EXPERTS:] * (carried + total[:N_EXPERTS] + before[N_EXPERTS:])
    rank = jnp.concatenate([jnp.sum(first, axis=0, keepdims=True), jnp.sum(second, axis=0, keepdims=True)], axis=0)
    cnt_scr[...] = jnp.broadcast_to(carried + total[:N_EXPERTS] + total[N_EXPERTS:], cnt_scr.shape)
    return rank.astype(jnp.int32)


def _merge_kernel(x_ref, mod_ref, ya_ref, ybp_ref, ybs_ref, ycp_ref, ycs_ref, g_refs, wa_ref, wb_ref, wc_ref,
                  wo_ref, nw_ref, rw_ref, rb_ref, x1_ref, h2_ref, eidx_ref, wts_ref, rank_ref, cnt_ref, cnt_scr,
                  *, first_latent_tile):
    half = COL_BLK
    tile = pl.program_id(0)

    @pl.when(tile == 0)
    def _():
        cnt_scr[...] = jnp.zeros_like(cnt_scr)

    is_ctx = tile < first_latent_tile
    pa = _dot(ya_ref[...], wa_ref[...])
    pb = _dot(jnp.where(is_ctx, ybp_ref[...], ybs_ref[...]), wb_ref[...])
    pc = _dot(jnp.where(is_ctx, ycp_ref[...], ycs_ref[...]), wc_ref[...])
    parts = []
    for c in range(2):
        cols = slice(c * half, (c + 1) * half)
        m = (_sigmoid(g_refs[0 + c][...].astype(F32)) * pa[:, cols]
             + _sigmoid(g_refs[2 + c][...].astype(F32)) * pb[:, cols]
             + _sigmoid(g_refs[4 + c][...].astype(F32)) * pc[:, cols])
        parts.append(m.astype(BF16))
    merged = jnp.concatenate(parts, axis=1)
    x1 = x_ref[...] + mod_ref[0, 2:3, :] * _dot(merged, wo_ref[...])
    x1_ref[...] = x1
    h2 = _rms(x1) * nw_ref[...] * (1.0 + mod_ref[0, 4:5, :]) + mod_ref[0, 3:4, :]
    h2_ref[...] = h2
    hh, hl = _split_bf16(h2)
    rh, rl = _split_bf16(rw_ref[...])
    logits = (lax.dot_general(rh, hh, NT_DIMS, preferred_element_type=F32)
              + lax.dot_general(rh, hl, NT_DIMS, preferred_element_type=F32)
              + lax.dot_general(rl, hh, NT_DIMS, preferred_element_type=F32))
    s = _sigmoid(logits)
    eidx, wts = _route(s + rb_ref[...], s)
    eidx_ref[...] = eidx
    wts_ref[...] = wts
    rank_ref[...] = _assignment_ranks(eidx, cnt_scr)
    cnt_ref[...] = cnt_scr[...]


def _merge(x, mod, mod_row, ya, yb_ctx, yb_lat, yc_ctx, yc_lat, z, w_br_a, w_br_b, w_br_c, w_out, norm2_w,
           router_wt, router_b, tm):
    t, d = x.shape
    n_ctx = yb_ctx.shape[0] // tm
    last_ctx = n_ctx - 1
    const = lambda shape: pl.BlockSpec(shape, lambda i: (0,) * len(shape))
    ctx_rows = lambda i: (jnp.minimum(i, last_ctx), 0)
    lat_rows = lambda i: (jnp.maximum(i - n_ctx, 0), 0)
    g_specs = [pl.BlockSpec((tm, COL_BLK), functools.partial(lambda i, cb: (i, cb), cb=CB_G + n)) for n in range(6)]

    def body(x_ref, mod_ref, ya_ref, ybp_ref, ybs_ref, ycp_ref, ycs_ref, g0, g1, g2, g3, g4, g5, *rest):
        _merge_kernel(x_ref, mod_ref, ya_ref, ybp_ref, ybs_ref, ycp_ref, ycs_ref, (g0, g1, g2, g3, g4, g5), *rest,
                      first_latent_tile=n_ctx)

    token_major = pl.BlockSpec((TOP_K, tm), lambda i: (0, i))
    return pl.pallas_call(
        body,
        grid=(t // tm,),
        in_specs=[pl.BlockSpec((tm, d), lambda i: (i, 0)),
                  pl.BlockSpec((1, 6, d), lambda i: (mod_row(i), 0, 0)),
                  pl.BlockSpec((tm, A_WIDTH), lambda i: (i, 0)),
                  pl.BlockSpec((tm, B_WIDTH), ctx_rows),
                  pl.BlockSpec((tm, B_WIDTH), lat_rows),
                  pl.BlockSpec((tm, C_WIDTH), ctx_rows),
                  pl.BlockSpec((tm, C_WIDTH), lat_rows),
                  *g_specs,
                  const(w_br_a.shape), const(w_br_b.shape), const(w_br_c.shape), const(w_out.shape),
                  const((1, d)), const(router_wt.shape), const((N_EXPERTS, 1))],
        out_specs=[pl.BlockSpec((tm, d), lambda i: (i, 0)),
                   pl.BlockSpec((tm, d), lambda i: (i, 0)),
                   token_major, token_major, token_major,
                   const((N_EXPERTS, LANES))],
        out_shape=[jax.ShapeDtypeStruct((t, d), F32),
                   jax.ShapeDtypeStruct((t, d), F32),
                   jax.ShapeDtypeStruct((TOP_K, t), jnp.int32),
                   jax.ShapeDtypeStruct((TOP_K, t), F32),
                   jax.ShapeDtypeStruct((TOP_K, t), jnp.int32),
                   jax.ShapeDtypeStruct((N_EXPERTS, LANES), F32)],
        scratch_shapes=[pltpu.VMEM((N_EXPERTS, LANES), F32)],
        compiler_params=_params("arbitrary"),
        name="merge_router",
    )(x, mod, ya, yb_ctx, yb_lat, yc_ctx, yc_lat, z, z, z, z, z, z, w_br_a, w_br_b, w_br_c, w_out, norm2_w,
      router_wt, router_b)


def _expert_kernel(dest_ref, te_ref, nv_ref, nu_ref, h_hbm, wg_ref, wu_ref, wd_ref, y_ref, src_ref, hbuf0, hbuf1, sem,
                   *, tm, n_tok):
    i = pl.program_id(0)
    n_used = nu_ref[0]
    bufs = (hbuf0, hbuf1)

    def row_copy(tile, r, slot):
        base = tile * tm
        last_valid = nv_ref[tile] - 1
        tok = src_ref[base + jnp.minimum(r, last_valid)]
        return pltpu.make_async_copy(h_hbm.at[pl.ds(tok, 1), :], bufs[slot].at[pl.ds(r, 1), :], sem.at[slot])

    def wait_rows(slot):
        pltpu.make_async_copy(bufs[slot], bufs[slot], sem.at[slot]).wait()

    @pl.when(i == 0)
    def _():
        def invert(t, carry):
            src_ref[dest_ref[t]] = t
            src_ref[dest_ref[n_tok + t]] = t
            return carry

        lax.fori_loop(0, n_tok, invert, 0, unroll=8)
        for r in range(tm):
            row_copy(0, r, 0).start()

    for slot in range(2):
        @pl.when(jnp.logical_and(i < n_used, i % 2 == slot))
        def _():
            wait_rows(slot)
            h = bufs[slot][...].astype(BF16)
            nxt = jnp.minimum(i + 1, n_used - 1)
            for r in range(tm):
                row_copy(nxt, r, 1 - slot).start()
            g = _dot(h, wg_ref[...].astype(BF16))
            a = (g * _sigmoid(g)) * _dot(h, wu_ref[...].astype(BF16))
            y_ref[...] = _dot(a.astype(BF16), wd_ref[...].astype(BF16)).astype(y_ref.dtype)

            @pl.when(i == n_used - 1)
            def _():
                wait_rows(1 - slot)

    @pl.when(i >= n_used)
    def _():
        y_ref[...] = jnp.zeros_like(y_ref)


def _experts(h2, dest, tile_expert, valid_rows, n_used, w_g, w_u, w_d, layer, tm):
    n_tok, d = h2.shape
    de = w_g.shape[-1]
    n_tiles = tile_expert.shape[0]
    weight = lambda shape: pl.BlockSpec((None, None) + shape, lambda i, dst, te, nv, nu: (layer, te[i], 0, 0))
    grid_spec = pltpu.PrefetchScalarGridSpec(
        num_scalar_prefetch=4,
        grid=(n_tiles,),
        in_specs=[pl.BlockSpec(memory_space=pl.ANY), weight((d, de)), weight((d, de)), weight((de, d))],
        out_specs=pl.BlockSpec((tm, d), lambda i, dst, te, nv, nu: (i, 0)),
        scratch_shapes=[pltpu.SMEM((n_tiles * tm,), jnp.int32),
                        pltpu.VMEM((tm, d), F32), pltpu.VMEM((tm, d), F32),
                        pltpu.SemaphoreType.DMA((2,))],
    )
    return pl.pallas_call(
        functools.partial(_expert_kernel, tm=tm, n_tok=n_tok),
        grid_spec=grid_spec,
        out_shape=jax.ShapeDtypeStruct((n_tiles * tm, d), BF16),
        compiler_params=_params("arbitrary"),
        name="experts",
    )(dest.reshape(TOP_K * n_tok), tile_expert, valid_rows, n_used, h2, w_g, w_u, w_d)


def _dispatch_plan(eidx, rank, counts, tm):
    t = eidx.shape[1]
    n_tiles = (TOP_K * t + N_EXPERTS * (tm - 1)) // tm
    tiles_per = (counts + tm - 1) // tm
    tile_end = jnp.cumsum(tiles_per)
    row_start = (tile_end - tiles_per) * tm
    experts = jnp.arange(N_EXPERTS, dtype=jnp.int32)
    start_of = jnp.sum(jnp.where(eidx[None] == experts[:, None, None], row_start[:, None, None], 0), axis=0)
    dest = rank + start_of
    tile_ids = jnp.arange(n_tiles, dtype=jnp.int32)
    tile_expert = jnp.minimum(jnp.sum((tile_end[None, :] <= tile_ids[:, None]).astype(jnp.int32), axis=1),
                              N_EXPERTS - 1)
    row_end = jnp.take(row_start + counts, tile_expert)
    valid_rows = jnp.clip(row_end - tile_ids * tm, 0, tm).astype(jnp.int32)
    n_used = tile_end[-1:].astype(jnp.int32)
    return dest, tile_expert.astype(jnp.int32), valid_rows, n_used


def _combine_kernel(x_ref, mod_ref, y0_ref, y1_ref, w_ref, o_ref):
    w = w_ref[...]
    y = w[:, 0:1] * y0_ref[...].astype(F32) + w[:, 1:2] * y1_ref[...].astype(F32)
    o_ref[...] = x_ref[...] + mod_ref[0, 5:6, :] * y


def _combine(x, mod, mod_row, y0, y1, w_cols, tm, row0, rows):
    d = x.shape[1]
    rb0 = row0 // tm
    token_rows = lambda width: pl.BlockSpec((tm, width), lambda i: (rb0 + i, 0))
    return pl.pallas_call(
        _combine_kernel,
        grid=(rows // tm,),
        in_specs=[token_rows(d),
                  pl.BlockSpec((1, 6, d), lambda i: (mod_row(rb0 + i), 0, 0)),
                  token_rows(d), token_rows(d), token_rows(TOP_K)],
        out_specs=pl.BlockSpec((tm, d), lambda i: (i, 0)),
        out_shape=jax.ShapeDtypeStruct((rows, d), F32),
        compiler_params=_params("parallel"),
        name="combine",
    )(x, mod, y0, y1, w_cols)


def _pick_tile(preferred, *sizes):
    tile = preferred
    while any(s % tile for s in sizes):
        tile //= 2
    return tile


def kernel(x_prompt, x_sample, cache_k, cache_v, c, c_ctx, w_mod, b_mod, norm1_w, norm2_w, w_in, a_norm_w, a_w_s, a_b_s, q_norm_w, k_norm_w, lambda_qk, subln_w, w_br_a, w_br_b, w_br_c, w_out, router_w, router_b, w_e_gate, w_e_up, w_e_down):
    batch, seq, d = x_prompt.shape
    dec_batch, dec_seq, _ = x_sample.shape
    depth = w_mod.shape[0]
    past = cache_k.shape[2]
    tp, ts = batch * seq, dec_batch * dec_seq
    t = tp + ts
    assert tp % dec_seq == 0 and dec_seq % seq == 0 and seq % CHUNK == 0 and dec_seq % GRID_W == 0

    tm = _pick_tile(1024, tp, dec_seq)
    tm_small = _pick_tile(512, tp, dec_seq)
    tm_exp = 512
    tn_in = 1536

    def mod_row_for(tile):
        first_lat = tp // tile
        per_batch = dec_seq // tile
        return lambda i: jnp.where(i < first_lat, 0, 1 + (i - first_lat) // per_batch)

    n_rows = 1 + dec_batch
    r_pad = -(-n_rows // 8) * 8
    cvec = jnp.zeros((r_pad, d), F32).at[0].set(c_ctx).at[1:n_rows].set(c)
    mod_all = _mod_vectors(cvec, w_mod, b_mod)[:, :n_rows].reshape(depth, n_rows, 6, d)

    rope_tabs = _rope_tables(dec_seq, tm)
    dft_p = _dft_tables(seq)
    dft_s = _dft_tables(dec_seq)
    cache_k2 = cache_k.reshape(dec_batch, depth, past, B_HEADS * 2 * B_QK_DIM)
    cache_v2 = cache_v.reshape(dec_batch, depth, past, B_WIDTH)
    router_wt = router_w.T
    router_b2 = router_b.reshape(N_EXPERTS, 1)

    x = jnp.concatenate([x_prompt.reshape(tp, d), x_sample.reshape(ts, d)], axis=0)
    new_k, new_v = [], []
    for l in range(depth):
        mod = mod_all[l]
        lam_init = 0.8 - 0.6 * math.exp(-0.3 * l)
        qw = jnp.tile(q_norm_w[l], 2).reshape(1, LANES)
        kw = jnp.tile(k_norm_w[l], 2).reshape(1, LANES)
        sw = subln_w[l].reshape(1, LANES)
        z, k32, v32 = _in_proj(x, mod, mod_row_for(tm), norm1_w[l].reshape(1, d), w_in[l].astype(BF16), qw, kw,
                               rope_tabs, tp // tm, dec_seq // tm, tm, tn_in)

        bias_a = jnp.repeat(a_b_s[l].T, A_GROUP_DIM, axis=1)
        ya = _mixer_a(z, a_norm_w[l].reshape(1, A_WIDTH), a_w_s[l].astype(BF16), bias_a, tm_small)

        yb_p = _attention(z, 0, batch, seq, lambda_qk[l], sw, lam_init, None, seq, B_HEADS)
        yb_s = _attention(z, tp, dec_batch, dec_seq, lambda_qk[l], sw, lam_init,
                          (cache_k2, cache_v2, l), _pick_tile(512, dec_seq), 1)

        yc_p = _fourier(z, 0, batch, seq, dft_p, seq)
        yc_s = _fourier(z, tp, dec_batch, dec_seq, dft_s, _pick_tile(512, dec_seq))

        x1, h2, eidx, wts, rank, cnt = _merge(
            x, mod, mod_row_for(tm_small), ya, yb_p, yb_s, yc_p, yc_s, z,
            w_br_a[l].astype(BF16), w_br_b[l].astype(BF16), w_br_c[l].astype(BF16), w_out[l].astype(BF16),
            norm2_w[l].reshape(1, d), router_wt, router_b2, tm_small)

        dest, tile_expert, valid_rows, n_used = _dispatch_plan(eidx, rank, cnt[:, 0].astype(jnp.int32), tm_exp)
        yg = _experts(h2, dest, tile_expert, valid_rows, n_used, w_e_gate, w_e_up, w_e_down, l, tm_exp)
        y0 = yg.at[dest[0]].get(mode="promise_in_bounds")
        y1 = yg.at[dest[1]].get(mode="promise_in_bounds")
        combine = functools.partial(_combine, x1, mod, mod_row_for(tm_small), y0, y1, wts.T, tm_small)
        if l + 1 < depth:
            x = combine(0, t)
        else:
            y_prompt = combine(0, tp).reshape(batch, seq, d)
            y_sample = combine(tp, ts).reshape(dec_batch, dec_seq, d)

        new_k.append(k32[:tp].reshape(batch, seq, B_HEADS, 2 * B_QK_DIM))
        new_v.append(v32[:tp].reshape(batch, seq, B_HEADS, B_V_DIM))

    return (y_prompt, y_sample, jnp.stack(new_k, axis=1), jnp.stack(new_v, axis=1))
```
